```python
import jax, jax.numpy as jnp
from jax import lax
import numpy as np

D_MODEL = 1024
BATCH = 16
SEQ = 2048
DEPTH = 1

D_MIX = D_MODEL
HEAD_DIM = 64
ATTN_WIDTH = D_MIX // 2
CONV_WIDTH = D_MIX - ATTN_WIDTH
N_HEADS = ATTN_WIDTH // HEAD_DIM
N_KV_HEADS = 2
GQA_GROUP = N_HEADS // N_KV_HEADS
KV_WIDTH = N_KV_HEADS * HEAD_DIM
N_CONV_GROUPS = CONV_WIDTH // HEAD_DIM
WINDOW = 128
CONV_K = 3
IN_WIDTH = ATTN_WIDTH + 2 * KV_WIDTH + 3 * CONV_WIDTH
N_KEYS = 128
N_EXPERTS = N_KEYS * N_KEYS
PEER_HEADS = 8
PEER_TOPK = 16
D_KEY = 256
D_HALF = D_KEY // 2
PEER_CHUNK = 128
EPS = 1e-6
NEG = -1e30

kernel_name = "hybrid_swa_sink_shortconv_peer_adaln"


def rms_norm(x, g):
    x32 = x.astype(jnp.float32)
    y = x32 * lax.rsqrt(jnp.mean(x32 * x32, axis=-1, keepdims=True) + EPS)
    return (y * g.astype(jnp.float32)).astype(x.dtype)


def head_rms_norm(x, g, n_groups):
    shp = x.shape
    xh = x.reshape(shp[:-1] + (n_groups, HEAD_DIM))
    y = rms_norm(xh, g.reshape(n_groups, HEAD_DIM))
    return y.reshape(shp)


def modulate(h, shift, scale):
    return h * (1 + scale[:, None, :]) + shift[:, None, :]


def sliding_window_sink_attention(q, k, v, sinks):
    bsz, s_len = q.shape[0], q.shape[1]
    nb = s_len // WINDOW
    qb = q.reshape(bsz, nb, WINDOW, N_KV_HEADS, GQA_GROUP, HEAD_DIM)
    kb = k.reshape(bsz, nb, WINDOW, N_KV_HEADS, HEAD_DIM)
    vb = v.reshape(bsz, nb, WINDOW, N_KV_HEADS, HEAD_DIM)

    def with_prev(t):
        prev = jnp.pad(t[:, :-1], ((0, 0), (1, 0), (0, 0), (0, 0), (0, 0)))
        return jnp.concatenate([prev, t], axis=2)

    kk, vv = with_prev(kb), with_prev(vb)
    s = jnp.einsum('bnqkgd,bnjkd->bnkgqj', qb, kk).astype(jnp.float32) * (HEAD_DIM ** -0.5)
    qi = jnp.arange(WINDOW)[:, None]
    kj = jnp.arange(2 * WINDOW)[None, :]
    diff = qi + WINDOW - kj
    rel_ok = (diff >= 0) & (diff < WINDOW)
    blk = jnp.arange(nb)[:, None, None]
    valid = rel_ok[None] & ((blk > 0) | (kj[None] >= WINDOW))
    s = jnp.where(valid[None, :, None, None], s, NEG)
    sink = sinks.astype(jnp.float32).reshape(N_KV_HEADS, GQA_GROUP)[None, None, :, :, None, None]
    m = jnp.maximum(jnp.max(s, axis=-1, keepdims=True), sink)
    e = jnp.exp(s - m)
    p = e / (jnp.sum(e, axis=-1, keepdims=True) + jnp.exp(sink - m))
    o = jnp.einsum('bnkgqj,bnjkd->bnqkgd', p.astype(v.dtype), vv)
    return o.reshape(bsz, s_len, ATTN_WIDTH)


def short_gated_conv(b_gate, c_gate, u, conv_w):
    z = c_gate * u
    zp = jnp.pad(z, ((0, 0), (CONV_K - 1, 0), (0, 0)))
    s_len = z.shape[1]
    conv = zp[:, 0:s_len] * conv_w[0] + zp[:, 1:s_len + 1] * conv_w[1] + zp[:, 2:s_len + 2] * conv_w[2]
    return b_gate * conv


def peer_layer(h, w_query, peer_keys1, peer_keys2, peer_u, peer_v):
    bsz, s_len, d = h.shape
    t = bsz * s_len
    hf = h.reshape(t, d)
    q = (hf @ w_query).reshape(t, PEER_HEADS, D_KEY)
    s1 = jnp.einsum('thd,hnd->thn', q[..., :D_HALF], peer_keys1).astype(jnp.float32)
    s2 = jnp.einsum('thd,hnd->thn', q[..., D_HALF:], peer_keys2).astype(jnp.float32)
    v1, i1 = lax.top_k(s1, PEER_TOPK)
    v2, i2 = lax.top_k(s2, PEER_TOPK)
    cand = (v1[..., :, None] + v2[..., None, :]).reshape(t, PEER_HEADS, PEER_TOPK * PEER_TOPK)
    sc, ci = lax.top_k(cand, PEER_TOPK)
    e1 = jnp.take_along_axis(i1, ci // PEER_TOPK, axis=-1)
    e2 = jnp.take_along_axis(i2, ci % PEER_TOPK, axis=-1)
    idx = e1 * N_KEYS + e2
    gates = jax.nn.softmax(sc, axis=-1).astype(h.dtype)

    nchunk = t // PEER_CHUNK
    hc = hf.reshape(nchunk, PEER_CHUNK, d)
    ic = idx.reshape(nchunk, PEER_CHUNK, PEER_HEADS, PEER_TOPK)
    gc = gates.reshape(nchunk, PEER_CHUNK, PEER_HEADS, PEER_TOPK)

    def chunk_fn(args):
        hx, ix, gx = args
        u = peer_u[ix]
        a = jax.nn.gelu(jnp.einsum('td,thkd->thk', hx, u), approximate=False)
        vsel = peer_v[ix]
        return jnp.einsum('thk,thkd->td', gx * a, vsel)

    out = lax.map(chunk_fn, (hc, ic, gc))
    return out.reshape(bsz, s_len, d)


def setup_inputs(seed: int = 0) -> dict:
    key = jax.random.key(seed)
    ks = jax.random.split(key, 20)
    D = D_MODEL
    f32 = jnp.float32
    return {
        "x": jax.random.normal(ks[0], (BATCH, SEQ, D), f32),
        "c": jax.random.normal(ks[1], (BATCH, D), f32),
        "w_ada": jax.random.normal(ks[2], (D, 6 * D), f32) * D ** -0.5,
        "b_ada": jax.random.normal(ks[3], (6 * D,), f32) * 0.02,
        "norm1_g": 1.0 + 0.02 * jax.random.normal(ks[4], (D,), f32),
        "w_in": jax.random.normal(ks[5], (D, IN_WIDTH), f32) * D ** -0.5,
        "b_in": jax.random.normal(ks[6], (IN_WIDTH,), f32) * 0.02,
        "attn_sinks": jax.random.normal(ks[7], (N_HEADS,), f32) * 0.5,
        "conv_w": jax.random.normal(ks[8], (CONV_K, CONV_WIDTH), f32) * CONV_K ** -0.5,
        "attn_out_g": 1.0 + 0.02 * jax.random.normal(ks[9], (ATTN_WIDTH,), f32),
        "conv_out_g": 1.0 + 0.02 * jax.random.normal(ks[10], (CONV_WIDTH,), f32),
        "w_out": jax.random.normal(ks[11], (D_MIX, D), f32) * D_MIX ** -0.5,
        "b_out": jax.random.normal(ks[12], (D,), f32) * 0.02,
        "norm2_g": 1.0 + 0.02 * jax.random.normal(ks[13], (D,), f32),
        "w_query": jax.random.normal(ks[14], (D, PEER_HEADS * D_KEY), f32) * D ** -0.5,
        "peer_keys1": jax.random.normal(ks[15], (PEER_HEADS, N_KEYS, D_HALF), f32) * D_HALF ** -0.5,
        "peer_keys2": jax.random.normal(ks[16], (PEER_HEADS, N_KEYS, D_HALF), f32) * D_HALF ** -0.5,
        "peer_u": jax.random.normal(ks[17], (N_EXPERTS, D), f32) * D ** -0.5,
        "peer_v": jax.random.normal(ks[18], (N_EXPERTS, D), f32) * PEER_HEADS ** -0.5,
        "final_g": 1.0 + 0.02 * jax.random.normal(ks[19], (D,), f32),
    }


def reference(x, c, w_ada, b_ada, norm1_g, w_in, b_in, attn_sinks, conv_w, attn_out_g,
              conv_out_g, w_out, b_out, norm2_g, w_query, peer_keys1, peer_keys2,
              peer_u, peer_v, final_g):
    bsz, s_len, _ = x.shape
    for _layer in range(DEPTH):
        ada = jax.nn.silu(c) @ w_ada + b_ada
        shift1, scale1, gate1, shift2, scale2, gate2 = jnp.split(ada, 6, axis=-1)

        h = modulate(rms_norm(x, norm1_g), shift1, scale1)
        proj = h @ w_in + b_in
        o1 = ATTN_WIDTH
        o2 = o1 + KV_WIDTH
        o3 = o2 + KV_WIDTH
        o4 = o3 + CONV_WIDTH
        o5 = o4 + CONV_WIDTH
        q = proj[..., :o1].reshape(bsz, s_len, N_HEADS, HEAD_DIM)
        k = proj[..., o1:o2].reshape(bsz, s_len, N_KV_HEADS, HEAD_DIM)
        v = proj[..., o2:o3].reshape(bsz, s_len, N_KV_HEADS, HEAD_DIM)
        b_gate = proj[..., o3:o4]
        c_gate = proj[..., o4:o5]
        u_conv = proj[..., o5:]
        attn = head_rms_norm(sliding_window_sink_attention(q, k, v, attn_sinks), attn_out_g, N_HEADS)
        conv = head_rms_norm(short_gated_conv(b_gate, c_gate, u_conv, conv_w), conv_out_g, N_CONV_GROUPS)
        mix = jnp.concatenate([attn, conv], axis=-1) @ w_out + b_out
        x = x + gate1[:, None, :] * mix

        h2 = modulate(rms_norm(x, norm2_g), shift2, scale2)
        x = x + gate2[:, None, :] * peer_layer(h2, w_query, peer_keys1, peer_keys2, peer_u, peer_v)
    return rms_norm(x, final_g)
```

```python
import functools

import jax
import jax.numpy as jnp
import numpy as np
from jax import lax
from jax.experimental import pallas as pl
from jax.experimental.pallas import tpu as pltpu

LANES = 128
SUBLANES = 8

D_MODEL = 1024
HEAD_DIM = 64
ATTN_WIDTH = 512
CONV_WIDTH = 512
N_HEADS = 8
N_KV_HEADS = 2
WINDOW = 128
N_KEYS = 128
N_EXPERTS = N_KEYS * N_KEYS
PEER_HEADS = 8
PEER_TOPK = 16
D_KEY = 256
D_HALF = 128
N_SEL = PEER_HEADS * PEER_TOPK
EPS = 1e-6
NEG = -1e30

P_Q = 0
P_K = 512
P_V = 768
P_B = 1024
P_C = 1536
P_U = 2048
P_W = 2560

CHUNKS = D_MODEL // LANES
WORD_ROWS = CHUNKS // 2

TOKEN_TILE = 512
PEER_TILE = 128

VMEM_LIMIT_SMALL = 48 * 1024 * 1024
VMEM_LIMIT_TABLE = 56 * 1024 * 1024


def _bf16(x):
    return x.astype(jnp.bfloat16)


def _dot(a, b):
    return jnp.dot(a, b, preferred_element_type=jnp.float32)


def _dot_nt(a, b):
    return lax.dot_general(a, b, (((1,), (1,)), ((), ())),
                           preferred_element_type=jnp.float32)


def _split2(x):
    hi = _bf16(x)
    lo = _bf16(x - hi.astype(jnp.float32))
    return hi, lo


def _split3(x):
    hi = _bf16(x)
    r = x - hi.astype(jnp.float32)
    mid = _bf16(r)
    lo = _bf16(r - mid.astype(jnp.float32))
    return hi, mid, lo


def _dot_sel(x, sel):
    hi, mid, lo = _split3(x)
    return _dot(hi, sel) + _dot(mid, sel) + _dot(lo, sel)


def _rms(x):
    return x * lax.rsqrt(jnp.mean(x * x, axis=-1, keepdims=True) + EPS)


def _ada_kernel(c_ref, w_ref, b_ref, o_ref):
    c = c_ref[...]
    a = c * (1.0 / (1.0 + jnp.exp(-c)))
    hi, mid, lo = _split3(a)
    w = w_ref[...]
    w_hi, w_mid, w_lo = _split3(w)
    acc = _dot(hi, w_hi)
    acc += _dot(hi, w_mid) + _dot(mid, w_hi)
    acc += _dot(hi, w_lo) + _dot(mid, w_mid) + _dot(lo, w_hi)
    o_ref[...] = acc + b_ref[...]


def _ada(c, w_ada, b_ada):
    bsz, d = c.shape
    n = w_ada.shape[1]
    blk = d
    return pl.pallas_call(
        _ada_kernel,
        grid=(n // blk,),
        in_specs=[
            pl.BlockSpec((bsz, d), lambda i: (0, 0)),
            pl.BlockSpec((d, blk), lambda i: (0, i)),
            pl.BlockSpec((1, blk), lambda i: (0, i)),
        ],
        out_specs=pl.BlockSpec((bsz, blk), lambda i: (0, i)),
        out_shape=jax.ShapeDtypeStruct((bsz, n), jnp.float32),
        compiler_params=pltpu.CompilerParams(
            dimension_semantics=("arbitrary",),
            vmem_limit_bytes=VMEM_LIMIT_SMALL),
        name="ada",
    )(c, w_ada, b_ada.reshape(1, n))


def _head_rms(o, gain, bd):
    hi, lo = _split2(o * o)
    ms = _dot(hi, bd) + _dot(lo, bd)
    return o * lax.rsqrt(ms + EPS) * gain


def _mixer_kernel(x_ref, sh_ref, sc_ref, gt_ref, g1_ref, win_ref, bin_ref,
                  sink_ref, cw_ref, ag_ref, cg_ref, wout_ref, bout_ref, bd_ref,
                  x1_ref, proj_ref, mix_ref, kc_ref, vc_ref, zc_ref):
    j = pl.program_id(1)
    tm = x_ref.shape[1]

    @pl.when(j == 0)
    def _():
        kc_ref[...] = jnp.zeros_like(kc_ref)
        vc_ref[...] = jnp.zeros_like(vc_ref)
        zc_ref[...] = jnp.zeros_like(zc_ref)

    x = x_ref[0]
    h = _rms(x) * g1_ref[...] * (1.0 + sc_ref[0]) + sh_ref[0]
    proj_ref[...] = _dot(_bf16(h), win_ref[...]) + bin_ref[...]

    qi = lax.broadcasted_iota(jnp.int32, (WINDOW, 2 * WINDOW), 0)
    kj = lax.broadcasted_iota(jnp.int32, (WINDOW, 2 * WINDOW), 1)
    valid = (kj > qi) & (kj <= qi + WINDOW)
    valid_first = valid & (kj >= jnp.where(j > 0, 0, WINDOW))
    lane_lo = lax.broadcasted_iota(jnp.int32, (1, LANES), 1) < HEAD_DIM
    scale = HEAD_DIM ** -0.5

    k_prev = _bf16(kc_ref[...])
    v_prev = _bf16(vc_ref[...])
    for n in range(tm // WINDOW):
        r0 = n * WINDOW
        k_cur = _bf16(proj_ref[r0:r0 + WINDOW, P_K:P_K + 2 * LANES])
        v_cur = _bf16(proj_ref[r0:r0 + WINDOW, P_V:P_V + 2 * LANES])
        kk = jnp.concatenate([k_prev, k_cur], axis=0)
        vv = jnp.concatenate([v_prev, v_cur], axis=0)
        ok = valid_first if n == 0 else valid
        for kvh in range(N_KV_HEADS):
            kt = kk[:, kvh * LANES:(kvh + 1) * LANES]
            vt = vv[:, kvh * LANES:(kvh + 1) * LANES]
            zero = jnp.zeros_like(kt)
            k_half = (jnp.where(lane_lo, kt, zero), jnp.where(lane_lo, zero, kt))
            v_half = (jnp.where(lane_lo, vt, zero), jnp.where(lane_lo, zero, vt))
            for a in (2 * kvh, 2 * kvh + 1):
                qt = _bf16(proj_ref[r0:r0 + WINDOW, P_Q + a * LANES:P_Q + (a + 1) * LANES])
                o_tile = None
                for par in range(2):
                    sink = sink_ref[2 * a + par]
                    s = _dot_nt(qt, k_half[par]) * scale
                    s = jnp.where(ok, s, NEG)
                    m = jnp.maximum(jnp.max(s, axis=-1, keepdims=True), sink)
                    e = jnp.exp(s - m)
                    den = jnp.sum(e, axis=-1, keepdims=True) + jnp.exp(sink - m)
                    p = _bf16(e * (1.0 / den))
                    o = _dot(p, v_half[par])
                    o_tile = o if o_tile is None else o_tile + o
                mix_ref[r0:r0 + WINDOW, a * LANES:(a + 1) * LANES] = o_tile
        k_prev, v_prev = k_cur, v_cur
    kc_ref[...] = k_prev.astype(jnp.float32)
    vc_ref[...] = v_prev.astype(jnp.float32)

    z = proj_ref[:, P_C:P_C + CONV_WIDTH] * proj_ref[:, P_U:P_U + CONV_WIDTH]
    zc = zc_ref[...]
    row8 = lax.broadcasted_iota(jnp.int32, (SUBLANES, CONV_WIDTH), 0)
    z1 = pltpu.roll(z, 1, axis=0)
    z2 = pltpu.roll(z, 2, axis=0)
    z1_top = jnp.where(row8 < 1, pltpu.roll(zc, 1, axis=0), z1[0:SUBLANES])
    z2_top = jnp.where(row8 < 2, pltpu.roll(zc, 2, axis=0), z2[0:SUBLANES])
    z1 = jnp.concatenate([z1_top, z1[SUBLANES:]], axis=0)
    z2 = jnp.concatenate([z2_top, z2[SUBLANES:]], axis=0)
    zc_ref[...] = z[tm - SUBLANES:tm]
    conv = z2 * cw_ref[0:1, :] + z1 * cw_ref[1:2, :] + z * cw_ref[2:3, :]
    conv = proj_ref[:, P_B:P_B + CONV_WIDTH] * conv

    bd = bd_ref[...]
    mix_ref[:, 0:ATTN_WIDTH] = _head_rms(mix_ref[:, 0:ATTN_WIDTH], ag_ref[...], bd)
    mix_ref[:, ATTN_WIDTH:] = _head_rms(conv, cg_ref[...], bd)
    mix = _dot(_bf16(mix_ref[...]), wout_ref[...]) + bout_ref[...]
    x1_ref[0] = x + gt_ref[0] * mix


def _mixer(x, shift1, scale1, gate1, norm1_g, w_in_r, b_in_r, sinks, conv_w,
           attn_out_g, conv_out_g, w_out_b, b_out):
    bsz, s_len, d = x.shape
    tm = min(TOKEN_TILE, s_len)
    bd = np.kron(np.eye(ATTN_WIDTH // HEAD_DIM, dtype=np.float32),
                 np.full((HEAD_DIM, HEAD_DIM), 1.0 / HEAD_DIM, np.float32))
    bd = jnp.asarray(bd, jnp.bfloat16)
    vec = lambda n: pl.BlockSpec((1, n), lambda b, j: (0, 0))
    per_b = pl.BlockSpec((1, 1, d), lambda b, j: (b, 0, 0))
    full = lambda shp: pl.BlockSpec(shp, lambda b, j: (0,) * len(shp))
    return pl.pallas_call(
        _mixer_kernel,
        grid=(bsz, s_len // tm),
        in_specs=[
            pl.BlockSpec((1, tm, d), lambda b, j: (b, j, 0)),
            per_b, per_b, per_b,
            vec(d),
            full((d, P_W)), vec(P_W),
            pl.BlockSpec(memory_space=pltpu.SMEM),
            full((3, CONV_WIDTH)),
            vec(ATTN_WIDTH), vec(CONV_WIDTH),
            full((d, d)), vec(d),
            full((ATTN_WIDTH, ATTN_WIDTH)),
        ],
        out_specs=pl.BlockSpec((1, tm, d), lambda b, j: (b, j, 0)),
        out_shape=jax.ShapeDtypeStruct((bsz, s_len, d), jnp.float32),
        scratch_shapes=[
            pltpu.VMEM((tm, P_W), jnp.float32),
            pltpu.VMEM((tm, d), jnp.float32),
            pltpu.VMEM((WINDOW, 2 * LANES), jnp.float32),
            pltpu.VMEM((WINDOW, 2 * LANES), jnp.float32),
            pltpu.VMEM((SUBLANES, CONV_WIDTH), jnp.float32),
        ],
        compiler_params=pltpu.CompilerParams(
            dimension_semantics=("arbitrary", "arbitrary"),
            vmem_limit_bytes=VMEM_LIMIT_SMALL),
        name="mixer",
    )(x, shift1.reshape(bsz, 1, d), scale1.reshape(bsz, 1, d), gate1.reshape(bsz, 1, d),
      norm1_g.reshape(1, d), w_in_r, b_in_r.reshape(1, P_W), sinks, conv_w,
      attn_out_g.reshape(1, ATTN_WIDTH), conv_out_g.reshape(1, CONV_WIDTH),
      w_out_b, b_out.reshape(1, d), bd)


def _staircase():
    flat = np.zeros((80,), np.int32)
    ok = np.zeros((80,), bool)
    for a in range(8):
        for b in range(8):
            flat[a * 8 + b] = a * PEER_TOPK + b
            ok[a * 8 + b] = (a + 1) * (b + 1) <= PEER_TOPK
    for r in range(8):
        flat[64 + r] = 8 + r
        ok[64 + r] = True
        flat[72 + r] = (8 + r) * PEER_TOPK
        ok[72 + r] = True
    return flat, ok


def _extract_topk(s, ids, big):
    rows16 = lax.broadcasted_iota(jnp.int32, (PEER_TOPK, LANES), 0)
    vals = jnp.zeros((PEER_TOPK, LANES), jnp.float32)
    idxs = jnp.zeros((PEER_TOPK, LANES), jnp.int32)
    for r in range(PEER_TOPK):
        m = jnp.max(s, axis=0, keepdims=True)
        i = jnp.min(jnp.where(s == m, ids, big), axis=0, keepdims=True)
        vals = jnp.where(rows16 == r, m, vals)
        idxs = jnp.where(rows16 == r, i, idxs)
        s = jnp.where(ids == i, -jnp.inf, s)
    return vals, idxs


def _take_rows(table, sel):
    out = jnp.zeros_like(table)
    for a in range(PEER_TOPK):
        out = jnp.where(sel == a, table[a:a + 1, :], out)
    return out


def _router_kernel(x1_ref, sh_ref, sc_ref, g2_ref, wq_ref, k1_ref, k2_ref,
                   flat_ref, ok_ref, h2_ref, idx_ref, gate_ref,
                   st_ref, oi_ref, og_ref):
    tm = x1_ref.shape[1]
    nt = tm // LANES
    x1 = x1_ref[0]
    h2 = _rms(x1) * g2_ref[...] * (1.0 + sc_ref[0]) + sh_ref[0]
    h2_ref[0] = h2
    q = _dot(_bf16(h2), wq_ref[...])
    for hd in range(PEER_HEADS):
        for side, kref in enumerate((k1_ref, k2_ref)):
            c0 = hd * D_KEY + side * D_HALF
            st = _dot_nt(kref[hd], _bf16(q[:, c0:c0 + D_HALF]))
            for tt in range(nt):
                st_ref[2 * hd + side, tt] = st[:, tt * LANES:(tt + 1) * LANES]

    key_ids = lax.broadcasted_iota(jnp.int32, (N_KEYS, LANES), 0)
    flat = flat_ref[...]
    cand_ok = ok_ref[...] > 0

    def body(n, carry):
        hd = lax.div(n, nt)
        tt = lax.rem(n, nt)
        v1, i1 = _extract_topk(st_ref[2 * hd, tt], key_ids, N_KEYS)
        v2, i2 = _extract_topk(st_ref[2 * hd + 1, tt], key_ids, N_KEYS)
        parts = [v1[a:a + 1, :] + v2[0:SUBLANES, :] for a in range(SUBLANES)]
        parts.append(v1[0:1, :] + v2[SUBLANES:, :])
        parts.append(v1[SUBLANES:, :] + v2[0:1, :])
        cand = jnp.where(cand_ok, jnp.concatenate(parts, axis=0), -jnp.inf)
        sc, cf = _extract_topk(cand, flat, PEER_TOPK * PEER_TOPK)
        e1 = _take_rows(i1, cf >> 4)
        e2 = _take_rows(i2, cf & (PEER_TOPK - 1))
        e = jnp.exp(sc - sc[0:1, :])
        g = e * (1.0 / jnp.sum(e, axis=0, keepdims=True))
        r0 = pl.multiple_of(hd * PEER_TOPK, PEER_TOPK)
        oi_ref[tt, pl.ds(r0, PEER_TOPK), :] = (e1 * N_KEYS + e2) * WORD_ROWS
        og_ref[tt, pl.ds(r0, PEER_TOPK), :] = g
        return carry

    lax.fori_loop(0, PEER_HEADS * nt, body, 0)
    for tt in range(nt):
        idx_ref[tt * LANES:(tt + 1) * LANES, :] = oi_ref[tt].T
        gate_ref[tt * LANES:(tt + 1) * LANES, :] = og_ref[tt].T


def _router(x1, shift2, scale2, norm2_g, w_query_b, keys1_b, keys2_b):
    bsz, s_len, d = x1.shape
    tm = min(TOKEN_TILE, s_len)
    nt = tm // LANES
    nj = s_len // tm
    flat, ok = _staircase()
    flat = jnp.asarray(np.broadcast_to(flat[:, None], (80, LANES)), jnp.int32)
    ok = jnp.asarray(np.broadcast_to(ok[:, None], (80, LANES)), jnp.int32)
    per_b = pl.BlockSpec((1, 1, d), lambda b, j: (b, 0, 0))
    full = lambda shp: pl.BlockSpec(shp, lambda b, j: (0,) * len(shp))
    tok = lambda b, j: (b * nj + j, 0)
    return pl.pallas_call(
        _router_kernel,
        grid=(bsz, nj),
        in_specs=[
            pl.BlockSpec((1, tm, d), lambda b, j: (b, j, 0)),
            per_b, per_b,
            full((1, d)),
            full((d, PEER_HEADS * D_KEY)),
            full((PEER_HEADS, N_KEYS, D_HALF)),
            full((PEER_HEADS, N_KEYS, D_HALF)),
            full((80, LANES)), full((80, LANES)),
        ],
        out_specs=[
            pl.BlockSpec((1, tm, d), lambda b, j: (b, j, 0)),
            pl.BlockSpec((tm, N_SEL), tok),
            pl.BlockSpec((tm, N_SEL), tok),
        ],
        out_shape=[
            jax.ShapeDtypeStruct((bsz, s_len, d), jnp.float32),
            jax.ShapeDtypeStruct((bsz * s_len, N_SEL), jnp.int32),
            jax.ShapeDtypeStruct((bsz * s_len, N_SEL), jnp.float32),
        ],
        scratch_shapes=[
            pltpu.VMEM((2 * PEER_HEADS, nt, N_KEYS, LANES), jnp.float32),
            pltpu.VMEM((nt, N_SEL, LANES), jnp.int32),
            pltpu.VMEM((nt, N_SEL, LANES), jnp.float32),
        ],
        compiler_params=pltpu.CompilerParams(
            dimension_semantics=("arbitrary", "arbitrary"),
            vmem_limit_bytes=VMEM_LIMIT_SMALL),
        name="router",
    )(x1, shift2.reshape(bsz, 1, d), scale2.reshape(bsz, 1, d), norm2_g.reshape(1, d),
      w_query_b, keys1_b, keys2_b, flat, ok)


def _pack_table(t):
    e = t.shape[0]
    tb = lax.bitcast_convert_type(_bf16(t).reshape(e, WORD_ROWS, 2, LANES), jnp.uint16)
    lo = tb[:, :, 0, :].astype(jnp.uint32)
    hi = tb[:, :, 1, :].astype(jnp.uint32)
    words = lax.bitcast_convert_type(lo | (hi << 16), jnp.int32)
    return words.reshape(e * WORD_ROWS, LANES)


def _gather_rows(idx_ref, base, table_ref, tile_ref):
    for mi in range(N_SEL):
        row = pl.multiple_of(idx_ref[base + mi], WORD_ROWS)
        tile_ref[mi * WORD_ROWS:(mi + 1) * WORD_ROWS, :] = table_ref[pl.ds(row, WORD_ROWS), :]


def _chunk_mask():
    r = lax.broadcasted_iota(jnp.int32, (CHUNKS, N_SEL * CHUNKS), 0)
    c = lax.broadcasted_iota(jnp.int32, (CHUNKS, N_SEL * CHUNKS), 1)
    return (c % CHUNKS) == r


def _peer_u_kernel(idx_ref, h_ref, gate_ref, table_hbm, gsum_ref, w_ref,
                   table_ref, tile_a, tile_b, d_ref, sem):
    i = pl.program_id(0)
    tb = h_ref.shape[0]

    @pl.when(i == 0)
    def _():
        cp = pltpu.make_async_copy(table_hbm, table_ref, sem)
        cp.start()
        cp.wait()

    mask = _chunk_mask()

    def scores(t, tile_ref):
        hi, lo = _split2(h_ref[t])
        lhs = jnp.concatenate([hi, lo], axis=0)
        xt = pltpu.bitcast(tile_ref[...], jnp.bfloat16)
        y = _dot_nt(lhs, xt)
        yc = y[0:CHUNKS] + y[CHUNKS:]
        d_ref[pl.ds(t, 1), :] = jnp.sum(jnp.where(mask, yc, 0.0), axis=0, keepdims=True)

    _gather_rows(idx_ref, 0, table_ref, tile_a)

    def body(p, carry):
        t0 = 2 * p
        _gather_rows(idx_ref, (t0 + 1) * N_SEL, table_ref, tile_b)
        scores(t0, tile_a)
        t2 = jnp.minimum(t0 + 2, tb - 1)
        _gather_rows(idx_ref, t2 * N_SEL, table_ref, tile_a)
        scores(t0 + 1, tile_b)
        return carry

    lax.fori_loop(0, tb // 2, body, 0)
    s = _dot_sel(d_ref[...], gsum_ref[...])
    act = 0.5 * s * (1.0 + lax.erf(s * (2.0 ** -0.5)))
    w_ref[...] = gate_ref[...] * act


def _peer_v_kernel(idx_ref, w_ref, table_hbm, rep_ref, expand_ref, o_ref,
                   table_ref, tile_a, tile_b, lhs_ref, sem):
    i = pl.program_id(0)
    tb = w_ref.shape[0]

    @pl.when(i == 0)
    def _():
        cp = pltpu.make_async_copy(table_hbm, table_ref, sem)
        cp.start()
        cp.wait()

    w_hi, w_lo = _split2(w_ref[...])
    stacked = jnp.concatenate([w_hi, w_lo], axis=0)
    rep = _dot(rep_ref[...], stacked)
    wide = _dot(_bf16(rep), expand_ref[...])
    r = lax.broadcasted_iota(jnp.int32, wide.shape, 0)
    c = lax.broadcasted_iota(jnp.int32, wide.shape, 1)
    lhs_ref[...] = _bf16(jnp.where((r % CHUNKS) == (c % CHUNKS), wide, 0.0))

    def combine(t, tile_ref):
        r0 = pl.multiple_of(t * 2 * CHUNKS, 2 * CHUNKS)
        lhs = lhs_ref[pl.ds(r0, 2 * CHUNKS), :]
        xt = pltpu.bitcast(tile_ref[...], jnp.bfloat16)
        o = _dot(lhs, xt)
        o_ref[t] = o[0:CHUNKS] + o[CHUNKS:]

    _gather_rows(idx_ref, 0, table_ref, tile_a)

    def body(p, carry):
        t0 = 2 * p
        _gather_rows(idx_ref, (t0 + 1) * N_SEL, table_ref, tile_b)
        combine(t0, tile_a)
        t2 = jnp.minimum(t0 + 2, tb - 1)
        _gather_rows(idx_ref, t2 * N_SEL, table_ref, tile_a)
        combine(t0 + 1, tile_b)
        return carry

    lax.fori_loop(0, tb // 2, body, 0)


def _peer_scratch(tb):
    return [
        pltpu.VMEM((N_EXPERTS * WORD_ROWS, LANES), jnp.int32),
        pltpu.VMEM((N_SEL * WORD_ROWS, LANES), jnp.int32),
        pltpu.VMEM((N_SEL * WORD_ROWS, LANES), jnp.int32),
    ]


def _peer_u(idx_flat, h3, gates, table):
    t = h3.shape[0]
    tb = min(PEER_TILE, t)
    gsum = np.kron(np.eye(N_SEL, dtype=np.float32), np.ones((CHUNKS, 1), np.float32))
    gsum = jnp.asarray(gsum, jnp.bfloat16)
    return pl.pallas_call(
        _peer_u_kernel,
        grid=(t // tb,),
        in_specs=[
            pl.BlockSpec((tb * N_SEL,), lambda i: (i,), memory_space=pltpu.SMEM),
            pl.BlockSpec((tb, CHUNKS, LANES), lambda i: (i, 0, 0)),
            pl.BlockSpec((tb, N_SEL), lambda i: (i, 0)),
            pl.BlockSpec(memory_space=pl.ANY),
            pl.BlockSpec((N_SEL * CHUNKS, N_SEL), lambda i: (0, 0)),
        ],
        out_specs=pl.BlockSpec((tb, N_SEL), lambda i: (i, 0)),
        out_shape=jax.ShapeDtypeStruct((t, N_SEL), jnp.float32),
        scratch_shapes=_peer_scratch(tb) + [
            pltpu.VMEM((tb, N_SEL * CHUNKS), jnp.float32),
            pltpu.SemaphoreType.DMA,
        ],
        compiler_params=pltpu.CompilerParams(
            dimension_semantics=("arbitrary",),
            vmem_limit_bytes=VMEM_LIMIT_TABLE),
        name="peer_u",
    )(idx_flat, h3, gates, table, gsum)


def _peer_v(idx_flat, w, table):
    t = w.shape[0]
    tb = min(PEER_TILE, t)
    rep = np.zeros((tb * 2 * CHUNKS, 2 * tb), np.float32)
    for tok in range(tb):
        for part in range(2):
            r = tok * 2 * CHUNKS + part * CHUNKS
            rep[r:r + CHUNKS, part * tb + tok] = 1.0
    rep = jnp.asarray(rep, jnp.bfloat16)
    expand = np.kron(np.eye(N_SEL, dtype=np.float32), np.ones((1, CHUNKS), np.float32))
    expand = jnp.asarray(expand, jnp.bfloat16)
    return pl.pallas_call(
        _peer_v_kernel,
        grid=(t // tb,),
        in_specs=[
            pl.BlockSpec((tb * N_SEL,), lambda i: (i,), memory_space=pltpu.SMEM),
            pl.BlockSpec((tb, N_SEL), lambda i: (i, 0)),
            pl.BlockSpec(memory_space=pl.ANY),
            pl.BlockSpec((tb * 2 * CHUNKS, 2 * tb), lambda i: (0, 0)),
            pl.BlockSpec((N_SEL, N_SEL * CHUNKS), lambda i: (0, 0)),
        ],
        out_specs=pl.BlockSpec((tb, CHUNKS, LANES), lambda i: (i, 0, 0)),
        out_shape=jax.ShapeDtypeStruct((t, CHUNKS, LANES), jnp.float32),
        scratch_shapes=_peer_scratch(tb) + [
            pltpu.VMEM((tb * 2 * CHUNKS, N_SEL * CHUNKS), jnp.bfloat16),
            pltpu.SemaphoreType.DMA,
        ],
        compiler_params=pltpu.CompilerParams(
            dimension_semantics=("arbitrary",),
            vmem_limit_bytes=VMEM_LIMIT_TABLE),
        name="peer_v",
    )(idx_flat, w, table, rep, expand)


def _final_kernel(x1_ref, p_ref, gt_ref, g_ref, y_ref):
    x2 = x1_ref[0] + gt_ref[0] * p_ref[0]
    y_ref[0] = _rms(x2) * g_ref[...]


def _final(x1, peer, gate2, final_g):
    bsz, s_len, d = x1.shape
    tm = min(TOKEN_TILE, s_len)
    blk = pl.BlockSpec((1, tm, d), lambda b, j: (b, j, 0))
    return pl.pallas_call(
        _final_kernel,
        grid=(bsz, s_len // tm),
        in_specs=[blk, blk,
                  pl.BlockSpec((1, 1, d), lambda b, j: (b, 0, 0)),
                  pl.BlockSpec((1, d), lambda b, j: (0, 0))],
        out_specs=blk,
        out_shape=jax.ShapeDtypeStruct((bsz, s_len, d), jnp.float32),
        compiler_params=pltpu.CompilerParams(
            dimension_semantics=("arbitrary", "arbitrary"),
            vmem_limit_bytes=VMEM_LIMIT_SMALL),
        name="final",
    )(x1, peer, gate2.reshape(bsz, 1, d), final_g.reshape(1, d))


def _relayout_in_proj(w_in, b_in):
    def cols(a):
        q = a[..., 0:512]
        k = a[..., 512:640]
        v = a[..., 640:768]
        rest = a[..., 768:]
        dup = lambda t: jnp.concatenate(
            [t[..., 0:64], t[..., 0:64], t[..., 64:128], t[..., 64:128]], axis=-1)
        return jnp.concatenate([q, dup(k), dup(v), rest], axis=-1)
    return cols(w_in), cols(b_in)


def kernel(x, c, w_ada, b_ada, norm1_g, w_in, b_in, attn_sinks, conv_w, attn_out_g,
           conv_out_g, w_out, b_out, norm2_g, w_query, peer_keys1, peer_keys2,
           peer_u, peer_v, final_g):
    bsz, s_len, d = x.shape
    ada = _ada(c, w_ada, b_ada)
    shift1, scale1, gate1, shift2, scale2, gate2 = jnp.split(ada, 6, axis=-1)

    w_in_r, b_in_r = _relayout_in_proj(w_in, b_in)
    x1 = _mixer(x, shift1, scale1, gate1, norm1_g, _bf16(w_in_r), b_in_r, attn_sinks,
                conv_w, attn_out_g, conv_out_g, _bf16(w_out), b_out)

    h2, idx, gates = _router(x1, shift2, scale2, norm2_g, _bf16(w_query),
                             _bf16(peer_keys1), _bf16(peer_keys2))
    t = bsz * s_len
    idx_flat = idx.reshape(t * N_SEL)
    w = _peer_u(idx_flat, h2.reshape(t, CHUNKS, LANES), gates, _pack_table(peer_u))
    peer = _peer_v(idx_flat, w, _pack_table(peer_v))
    return _final(x1, peer.reshape(bsz, s_len, d), gate2, final_g)
```

```python
import functools

import jax
import jax.numpy as jnp
import numpy as np
from jax import lax
from jax.experimental import pallas as pl
from jax.experimental.pallas import tpu as pltpu

LANES = 128
SUBLANES = 8

D_MODEL = 1024
HEAD_DIM = 64
ATTN_WIDTH = 512
CONV_WIDTH = 512
N_HEADS = 8
N_KV_HEADS = 2
WINDOW = 128
N_KEYS = 128
N_EXPERTS = N_KEYS * N_KEYS
PEER_HEADS = 8
PEER_TOPK = 16
D_KEY = 256
D_HALF = 128
N_SEL = PEER_HEADS * PEER_TOPK
EPS = 1e-6
NEG = -1e30

P_Q = 0
P_K = 512
P_V = 768
P_B = 1024
P_C = 1536
P_U = 2048
P_W = 2560

CHUNKS = D_MODEL // LANES
WORD_ROWS = CHUNKS // 2

TOKEN_TILE = 512
PEER_TILE = 256
PEER_TILES = 16

VMEM_LIMIT_SMALL = 48 * 1024 * 1024
VMEM_LIMIT_TABLE = 56 * 1024 * 1024


def _bf16(x):
    return x.astype(jnp.bfloat16)


def _dot(a, b):
    return jnp.dot(a, b, preferred_element_type=jnp.float32)


def _dot_nt(a, b):
    return lax.dot_general(a, b, (((1,), (1,)), ((), ())),
                           preferred_element_type=jnp.float32)


def _split2(x):
    hi = _bf16(x)
    lo = _bf16(x - hi.astype(jnp.float32))
    return hi, lo


def _split3(x):
    hi = _bf16(x)
    r = x - hi.astype(jnp.float32)
    mid = _bf16(r)
    lo = _bf16(r - mid.astype(jnp.float32))
    return hi, mid, lo


def _dot_sel(x, sel):
    hi, mid, lo = _split3(x)
    return _dot(hi, sel) + _dot(mid, sel) + _dot(lo, sel)


def _rms(x):
    return x * lax.rsqrt(jnp.mean(x * x, axis=-1, keepdims=True) + EPS)


def _ada_kernel(c_ref, w_ref, b_ref, o_ref):
    c = c_ref[...]
    a = c * (1.0 / (1.0 + jnp.exp(-c)))
    hi, mid, lo = _split3(a)
    w = w_ref[...]
    w_hi, w_mid, w_lo = _split3(w)
    acc = _dot(hi, w_hi)
    acc += _dot(hi, w_mid) + _dot(mid, w_hi)
    acc += _dot(hi, w_lo) + _dot(mid, w_mid) + _dot(lo, w_hi)
    o_ref[...] = acc + b_ref[...]


def _ada(c, w_ada, b_ada):
    bsz, d = c.shape
    n = w_ada.shape[1]
    blk = d
    return pl.pallas_call(
        _ada_kernel,
        grid=(n // blk,),
        in_specs=[
            pl.BlockSpec((bsz, d), lambda i: (0, 0)),
            pl.BlockSpec((d, blk), lambda i: (0, i)),
            pl.BlockSpec((1, blk), lambda i: (0, i)),
        ],
        out_specs=pl.BlockSpec((bsz, blk), lambda i: (0, i)),
        out_shape=jax.ShapeDtypeStruct((bsz, n), jnp.float32),
        compiler_params=pltpu.CompilerParams(
            dimension_semantics=("arbitrary",),
            vmem_limit_bytes=VMEM_LIMIT_SMALL),
        name="ada",
    )(c, w_ada, b_ada.reshape(1, n))


def _head_rms(o, gain, bd):
    hi, lo = _split2(o * o)
    ms = _dot(hi, bd) + _dot(lo, bd)
    return o * lax.rsqrt(ms + EPS) * gain


def _mixer_kernel(x_ref, sh_ref, sc_ref, gt_ref, g1_ref, win_ref, bin_ref,
                  sink_ref, cw_ref, ag_ref, cg_ref, wout_ref, bout_ref, bd_ref,
                  x1_ref, proj_ref, mix_ref, kc_ref, vc_ref, zc_ref):
    j = pl.program_id(1)
    tm = x_ref.shape[1]

    @pl.when(j == 0)
    def _():
        kc_ref[...] = jnp.zeros_like(kc_ref)
        vc_ref[...] = jnp.zeros_like(vc_ref)
        zc_ref[...] = jnp.zeros_like(zc_ref)

    x = x_ref[0]
    h = _rms(x) * g1_ref[...] * (1.0 + sc_ref[0]) + sh_ref[0]
    proj_ref[...] = _dot(_bf16(h), win_ref[...]) + bin_ref[...]

    qi = lax.broadcasted_iota(jnp.int32, (WINDOW, 2 * WINDOW), 0)
    kj = lax.broadcasted_iota(jnp.int32, (WINDOW, 2 * WINDOW), 1)
    valid = (kj > qi) & (kj <= qi + WINDOW)
    valid_first = valid & (kj >= jnp.where(j > 0, 0, WINDOW))
    lane_lo = lax.broadcasted_iota(jnp.int32, (1, LANES), 1) < HEAD_DIM
    scale = HEAD_DIM ** -0.5

    k_prev = _bf16(kc_ref[...])
    v_prev = _bf16(vc_ref[...])
    for n in range(tm // WINDOW):
        r0 = n * WINDOW
        k_cur = _bf16(proj_ref[r0:r0 + WINDOW, P_K:P_K + 2 * LANES])
        v_cur = _bf16(proj_ref[r0:r0 + WINDOW, P_V:P_V + 2 * LANES])
        kk = jnp.concatenate([k_prev, k_cur], axis=0)
        vv = jnp.concatenate([v_prev, v_cur], axis=0)
        ok = valid_first if n == 0 else valid
        for kvh in range(N_KV_HEADS):
            kt = kk[:, kvh * LANES:(kvh + 1) * LANES]
            vt = vv[:, kvh * LANES:(kvh + 1) * LANES]
            zero = jnp.zeros_like(kt)
            k_half = (jnp.where(lane_lo, kt, zero), jnp.where(lane_lo, zero, kt))
            v_half = (jnp.where(lane_lo, vt, zero), jnp.where(lane_lo, zero, vt))
            for a in (2 * kvh, 2 * kvh + 1):
                qt = _bf16(proj_ref[r0:r0 + WINDOW, P_Q + a * LANES:P_Q + (a + 1) * LANES])
                o_tile = None
                for par in range(2):
                    sink = sink_ref[2 * a + par]
                    s = _dot_nt(qt, k_half[par]) * scale
                    s = jnp.where(ok, s, NEG)
                    m = jnp.maximum(jnp.max(s, axis=-1, keepdims=True), sink)
                    e = jnp.exp(s - m)
                    den = jnp.sum(e, axis=-1, keepdims=True) + jnp.exp(sink - m)
                    p = _bf16(e * (1.0 / den))
                    o = _dot(p, v_half[par])
                    o_tile = o if o_tile is None else o_tile + o
                mix_ref[r0:r0 + WINDOW, a * LANES:(a + 1) * LANES] = o_tile
        k_prev, v_prev = k_cur, v_cur
    kc_ref[...] = k_prev.astype(jnp.float32)
    vc_ref[...] = v_prev.astype(jnp.float32)

    z = proj_ref[:, P_C:P_C + CONV_WIDTH] * proj_ref[:, P_U:P_U + CONV_WIDTH]
    zc = zc_ref[...]
    row8 = lax.broadcasted_iota(jnp.int32, (SUBLANES, CONV_WIDTH), 0)
    z1 = pltpu.roll(z, 1, axis=0)
    z2 = pltpu.roll(z, 2, axis=0)
    z1_top = jnp.where(row8 < 1, pltpu.roll(zc, 1, axis=0), z1[0:SUBLANES])
    z2_top = jnp.where(row8 < 2, pltpu.roll(zc, 2, axis=0), z2[0:SUBLANES])
    z1 = jnp.concatenate([z1_top, z1[SUBLANES:]], axis=0)
    z2 = jnp.concatenate([z2_top, z2[SUBLANES:]], axis=0)
    zc_ref[...] = z[tm - SUBLANES:tm]
    conv = z2 * cw_ref[0:1, :] + z1 * cw_ref[1:2, :] + z * cw_ref[2:3, :]
    conv = proj_ref[:, P_B:P_B + CONV_WIDTH] * conv

    bd = bd_ref[...]
    mix_ref[:, 0:ATTN_WIDTH] = _head_rms(mix_ref[:, 0:ATTN_WIDTH], ag_ref[...], bd)
    mix_ref[:, ATTN_WIDTH:] = _head_rms(conv, cg_ref[...], bd)
    mix = _dot(_bf16(mix_ref[...]), wout_ref[...]) + bout_ref[...]
    x1_ref[0] = x + gt_ref[0] * mix


def _mixer(x, shift1, scale1, gate1, norm1_g, w_in_r, b_in_r, sinks, conv_w,
           attn_out_g, conv_out_g, w_out_b, b_out):
    bsz, s_len, d = x.shape
    tm = min(TOKEN_TILE, s_len)
    bd = np.kron(np.eye(ATTN_WIDTH // HEAD_DIM, dtype=np.float32),
                 np.full((HEAD_DIM, HEAD_DIM), 1.0 / HEAD_DIM, np.float32))
    bd = jnp.asarray(bd, jnp.bfloat16)
    vec = lambda n: pl.BlockSpec((1, n), lambda b, j: (0, 0))
    per_b = pl.BlockSpec((1, 1, d), lambda b, j: (b, 0, 0))
    full = lambda shp: pl.BlockSpec(shp, lambda b, j: (0,) * len(shp))
    return pl.pallas_call(
        _mixer_kernel,
        grid=(bsz, s_len // tm),
        in_specs=[
            pl.BlockSpec((1, tm, d), lambda b, j: (b, j, 0)),
            per_b, per_b, per_b,
            vec(d),
            full((d, P_W)), vec(P_W),
            pl.BlockSpec(memory_space=pltpu.SMEM),
            full((3, CONV_WIDTH)),
            vec(ATTN_WIDTH), vec(CONV_WIDTH),
            full((d, d)), vec(d),
            full((ATTN_WIDTH, ATTN_WIDTH)),
        ],
        out_specs=pl.BlockSpec((1, tm, d), lambda b, j: (b, j, 0)),
        out_shape=jax.ShapeDtypeStruct((bsz, s_len, d), jnp.float32),
        scratch_shapes=[
            pltpu.VMEM((tm, P_W), jnp.float32),
            pltpu.VMEM((tm, d), jnp.float32),
            pltpu.VMEM((WINDOW, 2 * LANES), jnp.float32),
            pltpu.VMEM((WINDOW, 2 * LANES), jnp.float32),
            pltpu.VMEM((SUBLANES, CONV_WIDTH), jnp.float32),
        ],
        compiler_params=pltpu.CompilerParams(
            dimension_semantics=("arbitrary", "arbitrary"),
            vmem_limit_bytes=VMEM_LIMIT_SMALL),
        name="mixer",
    )(x, shift1.reshape(bsz, 1, d), scale1.reshape(bsz, 1, d), gate1.reshape(bsz, 1, d),
      norm1_g.reshape(1, d), w_in_r, b_in_r.reshape(1, P_W), sinks, conv_w,
      attn_out_g.reshape(1, ATTN_WIDTH), conv_out_g.reshape(1, CONV_WIDTH),
      w_out_b, b_out.reshape(1, d), bd)


def _staircase():
    flat = np.zeros((80,), np.int32)
    ok = np.zeros((80,), bool)
    for a in range(8):
        for b in range(8):
            flat[a * 8 + b] = a * PEER_TOPK + b
            ok[a * 8 + b] = (a + 1) * (b + 1) <= PEER_TOPK
    for r in range(8):
        flat[64 + r] = 8 + r
        ok[64 + r] = True
        flat[72 + r] = (8 + r) * PEER_TOPK
        ok[72 + r] = True
    return flat, ok


def _extract_topk(s, ids, big):
    rows16 = lax.broadcasted_iota(jnp.int32, (PEER_TOPK, LANES), 0)
    vals = jnp.zeros((PEER_TOPK, LANES), jnp.float32)
    idxs = jnp.zeros((PEER_TOPK, LANES), jnp.int32)
    for r in range(PEER_TOPK):
        m = jnp.max(s, axis=0, keepdims=True)
        i = jnp.min(jnp.where(s == m, ids, big), axis=0, keepdims=True)
        vals = jnp.where(rows16 == r, m, vals)
        idxs = jnp.where(rows16 == r, i, idxs)
        s = jnp.where(ids == i, -jnp.inf, s)
    return vals, idxs


def _take_rows(table, sel):
    out = jnp.zeros_like(table)
    for a in range(PEER_TOPK):
        out = jnp.where(sel == a, table[a:a + 1, :], out)
    return out


def _router_kernel(x1_ref, sh_ref, sc_ref, g2_ref, wq_ref, k1_ref, k2_ref,
                   flat_ref, ok_ref, h2_ref, idx_ref, gate_ref,
                   st_ref, oi_ref, og_ref):
    tm = x1_ref.shape[1]
    nt = tm // LANES
    x1 = x1_ref[0]
    h2 = _rms(x1) * g2_ref[...] * (1.0 + sc_ref[0]) + sh_ref[0]
    h2_ref[0] = h2
    q = _dot(_bf16(h2), wq_ref[...])
    for hd in range(PEER_HEADS):
        for side, kref in enumerate((k1_ref, k2_ref)):
            c0 = hd * D_KEY + side * D_HALF
            st = _dot_nt(kref[hd], _bf16(q[:, c0:c0 + D_HALF]))
            for tt in range(nt):
                st_ref[2 * hd + side, tt] = st[:, tt * LANES:(tt + 1) * LANES]

    key_ids = lax.broadcasted_iota(jnp.int32, (N_KEYS, LANES), 0)
    flat = flat_ref[...]
    cand_ok = ok_ref[...] > 0

    def body(n, carry):
        hd = lax.div(n, nt)
        tt = lax.rem(n, nt)
        v1, i1 = _extract_topk(st_ref[2 * hd, tt], key_ids, N_KEYS)
        v2, i2 = _extract_topk(st_ref[2 * hd + 1, tt], key_ids, N_KEYS)
        parts = [v1[a:a + 1, :] + v2[0:SUBLANES, :] for a in range(SUBLANES)]
        parts.append(v1[0:1, :] + v2[SUBLANES:, :])
        parts.append(v1[SUBLANES:, :] + v2[0:1, :])
        cand = jnp.where(cand_ok, jnp.concatenate(parts, axis=0), -jnp.inf)
        sc, cf = _extract_topk(cand, flat, PEER_TOPK * PEER_TOPK)
        e1 = _take_rows(i1, cf >> 4)
        e2 = _take_rows(i2, cf & (PEER_TOPK - 1))
        e = jnp.exp(sc - sc[0:1, :])
        g = e * (1.0 / jnp.sum(e, axis=0, keepdims=True))
        r0 = pl.multiple_of(hd * PEER_TOPK, PEER_TOPK)
        oi_ref[tt, pl.ds(r0, PEER_TOPK), :] = (e1 * N_KEYS + e2) * WORD_ROWS
        og_ref[tt, pl.ds(r0, PEER_TOPK), :] = g
        return carry

    lax.fori_loop(0, PEER_HEADS * nt, body, 0)
    for tt in range(nt):
        idx_ref[tt * LANES:(tt + 1) * LANES, :] = oi_ref[tt].T
        gate_ref[tt * LANES:(tt + 1) * LANES, :] = og_ref[tt].T


def _router(x1, shift2, scale2, norm2_g, w_query_b, keys1_b, keys2_b):
    bsz, s_len, d = x1.shape
    tm = min(TOKEN_TILE, s_len)
    nt = tm // LANES
    nj = s_len // tm
    flat, ok = _staircase()
    flat = jnp.asarray(np.broadcast_to(flat[:, None], (80, LANES)), jnp.int32)
    ok = jnp.asarray(np.broadcast_to(ok[:, None], (80, LANES)), jnp.int32)
    per_b = pl.BlockSpec((1, 1, d), lambda b, j: (b, 0, 0))
    full = lambda shp: pl.BlockSpec(shp, lambda b, j: (0,) * len(shp))
    tok = lambda b, j: (b * nj + j, 0)
    return pl.pallas_call(
        _router_kernel,
        grid=(bsz, nj),
        in_specs=[
            pl.BlockSpec((1, tm, d), lambda b, j: (b, j, 0)),
            per_b, per_b,
            full((1, d)),
            full((d, PEER_HEADS * D_KEY)),
            full((PEER_HEADS, N_KEYS, D_HALF)),
            full((PEER_HEADS, N_KEYS, D_HALF)),
            full((80, LANES)), full((80, LANES)),
        ],
        out_specs=[
            pl.BlockSpec((1, tm, d), lambda b, j: (b, j, 0)),
            pl.BlockSpec((tm, N_SEL), tok),
            pl.BlockSpec((tm, N_SEL), tok),
        ],
        out_shape=[
            jax.ShapeDtypeStruct((bsz, s_len, d), jnp.float32),
            jax.ShapeDtypeStruct((bsz * s_len, N_SEL), jnp.int32),
            jax.ShapeDtypeStruct((bsz * s_len, N_SEL), jnp.float32),
        ],
        scratch_shapes=[
            pltpu.VMEM((2 * PEER_HEADS, nt, N_KEYS, LANES), jnp.float32),
            pltpu.VMEM((nt, N_SEL, LANES), jnp.int32),
            pltpu.VMEM((nt, N_SEL, LANES), jnp.float32),
        ],
        compiler_params=pltpu.CompilerParams(
            dimension_semantics=("arbitrary", "arbitrary"),
            vmem_limit_bytes=VMEM_LIMIT_SMALL),
        name="router",
    )(x1, shift2.reshape(bsz, 1, d), scale2.reshape(bsz, 1, d), norm2_g.reshape(1, d),
      w_query_b, keys1_b, keys2_b, flat, ok)


def _pack_table(t):
    e = t.shape[0]
    pairs = jnp.swapaxes(_bf16(t).reshape(e, WORD_ROWS, 2, LANES), -1, -2)
    return lax.bitcast_convert_type(pairs, jnp.int32).reshape(e * WORD_ROWS, LANES)


def _gather_rows(idx_ref, base, table_ref, tile_ref):
    for m0 in range(0, N_SEL, SUBLANES):
        ids = idx_ref.at[pl.ds(base + m0, SUBLANES)]
        for k in range(SUBLANES):
            mi = m0 + k
            row = pl.multiple_of(ids[k], WORD_ROWS)
            tile_ref[mi * WORD_ROWS:(mi + 1) * WORD_ROWS, :] = table_ref[pl.ds(row, WORD_ROWS), :]


def _chunk_mask():
    r = lax.broadcasted_iota(jnp.int32, (CHUNKS, N_SEL * CHUNKS), 0)
    c = lax.broadcasted_iota(jnp.int32, (CHUNKS, N_SEL * CHUNKS), 1)
    return (c % CHUNKS) == r


def _token_pipeline(idx_ref, table_ref, tiles, n_tokens, consume):
    g = len(tiles) // 2
    group_a, group_b = tiles[:g], tiles[g:]

    def gather_group(t, group):
        for k in range(g):
            tk = jnp.minimum(t + k, n_tokens - 1)
            _gather_rows(idx_ref, tk * N_SEL, table_ref, group[k])

    def consume_group(t, group):
        for k in range(g):
            consume(t + k, group[k])

    gather_group(0, group_a)
    always = idx_ref[0] >= 0

    def body(p, carry):
        t0 = 2 * g * p

        @pl.when(always)
        def _():
            consume_group(t0, group_a)
            gather_group(t0 + g, group_b)

        @pl.when(always)
        def _():
            consume_group(t0 + g, group_b)
            gather_group(t0 + 2 * g, group_a)

        return carry

    lax.fori_loop(0, n_tokens // (2 * g), body, 0)


def _load_table(table_hbm, table_ref, sem):
    @pl.when(pl.program_id(0) == 0)
    def _():
        cp = pltpu.make_async_copy(table_hbm, table_ref, sem)
        cp.start()
        cp.wait()


def _peer_u_kernel(idx_ref, h_ref, gate_ref, table_hbm, gsum_ref, w_ref,
                   table_ref, *scratch):
    tiles, (d_ref, sem) = scratch[:PEER_TILES], scratch[PEER_TILES:]
    tb = h_ref.shape[0]
    _load_table(table_hbm, table_ref, sem)
    mask = _chunk_mask()

    def scores(t, tile_ref):
        hi, lo = _split2(h_ref[t])
        lhs = jnp.concatenate([hi, lo], axis=0)
        xt = pltpu.bitcast(tile_ref[...], jnp.bfloat16)
        y = _dot_nt(lhs, xt)
        yc = y[0:CHUNKS] + y[CHUNKS:]
        d_ref[pl.ds(t, 1), :] = jnp.sum(jnp.where(mask, yc, 0.0), axis=0, keepdims=True)

    _token_pipeline(idx_ref, table_ref, tiles, tb, scores)
    s = _dot_sel(d_ref[...], gsum_ref[...])
    act = 0.5 * s * (1.0 + lax.erf(s * (2.0 ** -0.5)))
    w_ref[...] = gate_ref[...] * act


def _peer_v_kernel(idx_ref, w_ref, table_hbm, expand_ref, o_ref,
                   table_ref, *scratch):
    tiles, (whi_ref, wlo_ref, sem) = scratch[:PEER_TILES], scratch[PEER_TILES:]
    tb = w_ref.shape[0]
    _load_table(table_hbm, table_ref, sem)

    w_hi, w_lo = _split2(w_ref[...])
    whi_ref[...] = _dot(w_hi, expand_ref[...])
    wlo_ref[...] = _dot(w_lo, expand_ref[...])
    mask = _chunk_mask()

    def combine(t, tile_ref):
        hi = jnp.broadcast_to(whi_ref[pl.ds(t, 1), :], mask.shape)
        lo = jnp.broadcast_to(wlo_ref[pl.ds(t, 1), :], mask.shape)
        lhs = jnp.concatenate([_bf16(jnp.where(mask, hi, 0.0)),
                               _bf16(jnp.where(mask, lo, 0.0))], axis=0)
        xt = pltpu.bitcast(tile_ref[...], jnp.bfloat16)
        o = _dot(lhs, xt)
        o_ref[t] = o[0:CHUNKS] + o[CHUNKS:]

    _token_pipeline(idx_ref, table_ref, tiles, tb, combine)


def _peer_scratch(tb):
    tile = pltpu.VMEM((N_SEL * WORD_ROWS, LANES), jnp.int32)
    return [pltpu.VMEM((N_EXPERTS * WORD_ROWS, LANES), jnp.int32)] + [tile] * PEER_TILES


def _peer_u(idx_flat, h3, gates, table):
    t = h3.shape[0]
    tb = min(PEER_TILE, t)
    gsum = np.kron(np.eye(N_SEL, dtype=np.float32), np.ones((CHUNKS, 1), np.float32))
    gsum = jnp.asarray(gsum, jnp.bfloat16)
    return pl.pallas_call(
        _peer_u_kernel,
        grid=(t // tb,),
        in_specs=[
            pl.BlockSpec((tb * N_SEL,), lambda i: (i,), memory_space=pltpu.SMEM),
            pl.BlockSpec((tb, CHUNKS, LANES), lambda i: (i, 0, 0)),
            pl.BlockSpec((tb, N_SEL), lambda i: (i, 0)),
            pl.BlockSpec(memory_space=pl.ANY),
            pl.BlockSpec((N_SEL * CHUNKS, N_SEL), lambda i: (0, 0)),
        ],
        out_specs=pl.BlockSpec((tb, N_SEL), lambda i: (i, 0)),
        out_shape=jax.ShapeDtypeStruct((t, N_SEL), jnp.float32),
        scratch_shapes=_peer_scratch(tb) + [
            pltpu.VMEM((tb, N_SEL * CHUNKS), jnp.float32),
            pltpu.SemaphoreType.DMA,
        ],
        compiler_params=pltpu.CompilerParams(
            dimension_semantics=("arbitrary",),
            vmem_limit_bytes=VMEM_LIMIT_TABLE),
        name="peer_u",
    )(idx_flat, h3, gates, table, gsum)


def _peer_v(idx_flat, w, table):
    t = w.shape[0]
    tb = min(PEER_TILE, t)
    expand = np.kron(np.eye(N_SEL, dtype=np.float32), np.ones((1, CHUNKS), np.float32))
    expand = jnp.asarray(expand, jnp.bfloat16)
    return pl.pallas_call(
        _peer_v_kernel,
        grid=(t // tb,),
        in_specs=[
            pl.BlockSpec((tb * N_SEL,), lambda i: (i,), memory_space=pltpu.SMEM),
            pl.BlockSpec((tb, N_SEL), lambda i: (i, 0)),
            pl.BlockSpec(memory_space=pl.ANY),
            pl.BlockSpec((N_SEL, N_SEL * CHUNKS), lambda i: (0, 0)),
        ],
        out_specs=pl.BlockSpec((tb, CHUNKS, LANES), lambda i: (i, 0, 0)),
        out_shape=jax.ShapeDtypeStruct((t, CHUNKS, LANES), jnp.float32),
        scratch_shapes=_peer_scratch(tb) + [
            pltpu.VMEM((tb, N_SEL * CHUNKS), jnp.float32),
            pltpu.VMEM((tb, N_SEL * CHUNKS), jnp.float32),
            pltpu.SemaphoreType.DMA,
        ],
        compiler_params=pltpu.CompilerParams(
            dimension_semantics=("arbitrary",),
            vmem_limit_bytes=VMEM_LIMIT_TABLE),
        name="peer_v",
    )(idx_flat, w, table, expand)


def _final_kernel(x1_ref, p_ref, gt_ref, g_ref, y_ref):
    x2 = x1_ref[0] + gt_ref[0] * p_ref[0]
    y_ref[0] = _rms(x2) * g_ref[...]


def _final(x1, peer, gate2, final_g):
    bsz, s_len, d = x1.shape
    tm = min(TOKEN_TILE, s_len)
    blk = pl.BlockSpec((1, tm, d), lambda b, j: (b, j, 0))
    return pl.pallas_call(
        _final_kernel,
        grid=(bsz, s_len // tm),
        in_specs=[blk, blk,
                  pl.BlockSpec((1, 1, d), lambda b, j: (b, 0, 0)),
                  pl.BlockSpec((1, d), lambda b, j: (0, 0))],
        out_specs=blk,
        out_shape=jax.ShapeDtypeStruct((bsz, s_len, d), jnp.float32),
        compiler_params=pltpu.CompilerParams(
            dimension_semantics=("arbitrary", "arbitrary"),
            vmem_limit_bytes=VMEM_LIMIT_SMALL),
        name="final",
    )(x1, peer, gate2.reshape(bsz, 1, d), final_g.reshape(1, d))


def _relayout_in_proj(w_in, b_in):
    def cols(a):
        q = a[..., 0:512]
        k = a[..., 512:640]
        v = a[..., 640:768]
        rest = a[..., 768:]
        dup = lambda t: jnp.concatenate(
            [t[..., 0:64], t[..., 0:64], t[..., 64:128], t[..., 64:128]], axis=-1)
        return jnp.concatenate([q, dup(k), dup(v), rest], axis=-1)
    return cols(w_in), cols(b_in)


def kernel(x, c, w_ada, b_ada, norm1_g, w_in, b_in, attn_sinks, conv_w, attn_out_g,
           conv_out_g, w_out, b_out, norm2_g, w_query, peer_keys1, peer_keys2,
           peer_u, peer_v, final_g):
    bsz, s_len, d = x.shape
    ada = _ada(c, w_ada, b_ada)
    shift1, scale1, gate1, shift2, scale2, gate2 = jnp.split(ada, 6, axis=-1)

    w_in_r, b_in_r = _relayout_in_proj(w_in, b_in)
    x1 = _mixer(x, shift1, scale1, gate1, norm1_g, _bf16(w_in_r), b_in_r, attn_sinks,
                conv_w, attn_out_g, conv_out_g, _bf16(w_out), b_out)

    h2, idx, gates = _router(x1, shift2, scale2, norm2_g, _bf16(w_query),
                             _bf16(peer_keys1), _bf16(peer_keys2))
    t = bsz * s_len
    idx_flat = idx.reshape(t * N_SEL)
    w = _peer_u(idx_flat, h2.reshape(t, CHUNKS, LANES), gates, _pack_table(peer_u))
    peer = _peer_v(idx_flat, w, _pack_table(peer_v))
    return _final(x1, peer.reshape(bsz, s_len, d), gate2, final_g)
```

```python
import functools

import jax
import jax.numpy as jnp
import numpy as np
from jax import lax
from jax.experimental import pallas as pl
from jax.experimental.pallas import tpu as pltpu

LANES = 128
SUBLANES = 8

D_MODEL = 1024
HEAD_DIM = 64
ATTN_WIDTH = 512
CONV_WIDTH = 512
N_HEADS = 8
N_KV_HEADS = 2
WINDOW = 128
N_KEYS = 128
N_EXPERTS = N_KEYS * N_KEYS
PEER_HEADS = 8
PEER_TOPK = 16
D_KEY = 256
D_HALF = 128
N_SEL = PEER_HEADS * PEER_TOPK
EPS = 1e-6
NEG = -1e30

P_Q = 0
P_K = 512
P_V = 768
P_B = 1024
P_C = 1536
P_U = 2048
P_W = 2560

CHUNKS = D_MODEL // LANES
WORD_ROWS = CHUNKS // 2

TOKEN_TILE = 512
ROUTER_UNROLL = 4
PEER_TILE = 256
PEER_TILES = 16

VMEM_LIMIT_SMALL = 48 * 1024 * 1024
VMEM_LIMIT_TABLE = 56 * 1024 * 1024


def _bf16(x):
    return x.astype(jnp.bfloat16)


def _dot(a, b):
    return jnp.dot(a, b, preferred_element_type=jnp.float32)


def _dot_nt(a, b):
    return lax.dot_general(a, b, (((1,), (1,)), ((), ())),
                           preferred_element_type=jnp.float32)


def _split2(x):
    hi = _bf16(x)
    lo = _bf16(x - hi.astype(jnp.float32))
    return hi, lo


def _split3(x):
    hi = _bf16(x)
    r = x - hi.astype(jnp.float32)
    mid = _bf16(r)
    lo = _bf16(r - mid.astype(jnp.float32))
    return hi, mid, lo


def _dot_sel(x, sel):
    hi, mid, lo = _split3(x)
    return _dot(hi, sel) + _dot(mid, sel) + _dot(lo, sel)


def _rms(x):
    return x * lax.rsqrt(jnp.mean(x * x, axis=-1, keepdims=True) + EPS)


def _ada_kernel(c_ref, w_ref, b_ref, o_ref):
    c = c_ref[...]
    a = c * (1.0 / (1.0 + jnp.exp(-c)))
    hi, mid, lo = _split3(a)
    w = w_ref[...]
    w_hi, w_mid, w_lo = _split3(w)
    acc = _dot(hi, w_hi)
    acc += _dot(hi, w_mid) + _dot(mid, w_hi)
    acc += _dot(hi, w_lo) + _dot(mid, w_mid) + _dot(lo, w_hi)
    o_ref[...] = acc + b_ref[...]


def _ada(c, w_ada, b_ada):
    bsz, d = c.shape
    n = w_ada.shape[1]
    blk = d
    return pl.pallas_call(
        _ada_kernel,
        grid=(n // blk,),
        in_specs=[
            pl.BlockSpec((bsz, d), lambda i: (0, 0)),
            pl.BlockSpec((d, blk), lambda i: (0, i)),
            pl.BlockSpec((1, blk), lambda i: (0, i)),
        ],
        out_specs=pl.BlockSpec((bsz, blk), lambda i: (0, i)),
        out_shape=jax.ShapeDtypeStruct((bsz, n), jnp.float32),
        compiler_params=pltpu.CompilerParams(
            dimension_semantics=("arbitrary",),
            vmem_limit_bytes=VMEM_LIMIT_SMALL),
        name="ada",
    )(c, w_ada, b_ada.reshape(1, n))


def _head_rms(o, gain, bd):
    hi, lo = _split2(o * o)
    ms = _dot(hi, bd) + _dot(lo, bd)
    return o * lax.rsqrt(ms + EPS) * gain


def _mixer_kernel(x_ref, sh_ref, sc_ref, gt_ref, g1_ref, win_ref, bin_ref,
                  sink_ref, cw_ref, ag_ref, cg_ref, wout_ref, bout_ref, bd_ref,
                  x1_ref, proj_ref, mix_ref, kc_ref, vc_ref, zc_ref):
    j = pl.program_id(1)
    tm = x_ref.shape[1]

    @pl.when(j == 0)
    def _():
        kc_ref[...] = jnp.zeros_like(kc_ref)
        vc_ref[...] = jnp.zeros_like(vc_ref)
        zc_ref[...] = jnp.zeros_like(zc_ref)

    x = x_ref[0]
    h = _rms(x) * g1_ref[...] * (1.0 + sc_ref[0]) + sh_ref[0]
    proj_ref[...] = _dot(_bf16(h), win_ref[...]) + bin_ref[...]

    qi = lax.broadcasted_iota(jnp.int32, (WINDOW, 2 * WINDOW), 0)
    kj = lax.broadcasted_iota(jnp.int32, (WINDOW, 2 * WINDOW), 1)
    valid = (kj > qi) & (kj <= qi + WINDOW)
    valid_first = valid & (kj >= jnp.where(j > 0, 0, WINDOW))
    lane_lo = lax.broadcasted_iota(jnp.int32, (1, LANES), 1) < HEAD_DIM
    scale = HEAD_DIM ** -0.5

    k_prev = _bf16(kc_ref[...])
    v_prev = _bf16(vc_ref[...])
    for n in range(tm // WINDOW):
        r0 = n * WINDOW
        k_cur = _bf16(proj_ref[r0:r0 + WINDOW, P_K:P_K + 2 * LANES])
        v_cur = _bf16(proj_ref[r0:r0 + WINDOW, P_V:P_V + 2 * LANES])
        kk = jnp.concatenate([k_prev, k_cur], axis=0)
        vv = jnp.concatenate([v_prev, v_cur], axis=0)
        ok = valid_first if n == 0 else valid
        for kvh in range(N_KV_HEADS):
            kt = kk[:, kvh * LANES:(kvh + 1) * LANES]
            vt = vv[:, kvh * LANES:(kvh + 1) * LANES]
            zero = jnp.zeros_like(kt)
            k_half = (jnp.where(lane_lo, kt, zero), jnp.where(lane_lo, zero, kt))
            v_half = (jnp.where(lane_lo, vt, zero), jnp.where(lane_lo, zero, vt))
            for a in (2 * kvh, 2 * kvh + 1):
                qt = _bf16(proj_ref[r0:r0 + WINDOW, P_Q + a * LANES:P_Q + (a + 1) * LANES])
                o_tile = None
                for par in range(2):
                    sink = sink_ref[2 * a + par]
                    s = _dot_nt(qt, k_half[par]) * scale
                    s = jnp.where(ok, s, NEG)
                    m = jnp.maximum(jnp.max(s, axis=-1, keepdims=True), sink)
                    e = jnp.exp(s - m)
                    den = jnp.sum(e, axis=-1, keepdims=True) + jnp.exp(sink - m)
                    p = _bf16(e * (1.0 / den))
                    o = _dot(p, v_half[par])
                    o_tile = o if o_tile is None else o_tile + o
                mix_ref[r0:r0 + WINDOW, a * LANES:(a + 1) * LANES] = o_tile
        k_prev, v_prev = k_cur, v_cur
    kc_ref[...] = k_prev.astype(jnp.float32)
    vc_ref[...] = v_prev.astype(jnp.float32)

    z = proj_ref[:, P_C:P_C + CONV_WIDTH] * proj_ref[:, P_U:P_U + CONV_WIDTH]
    zc = zc_ref[...]
    row8 = lax.broadcasted_iota(jnp.int32, (SUBLANES, CONV_WIDTH), 0)
    z1 = pltpu.roll(z, 1, axis=0)
    z2 = pltpu.roll(z, 2, axis=0)
    z1_top = jnp.where(row8 < 1, pltpu.roll(zc, 1, axis=0), z1[0:SUBLANES])
    z2_top = jnp.where(row8 < 2, pltpu.roll(zc, 2, axis=0), z2[0:SUBLANES])
    z1 = jnp.concatenate([z1_top, z1[SUBLANES:]], axis=0)
    z2 = jnp.concatenate([z2_top, z2[SUBLANES:]], axis=0)
    zc_ref[...] = z[tm - SUBLANES:tm]
    conv = z2 * cw_ref[0:1, :] + z1 * cw_ref[1:2, :] + z * cw_ref[2:3, :]
    conv = proj_ref[:, P_B:P_B + CONV_WIDTH] * conv

    bd = bd_ref[...]
    mix_ref[:, 0:ATTN_WIDTH] = _head_rms(mix_ref[:, 0:ATTN_WIDTH], ag_ref[...], bd)
    mix_ref[:, ATTN_WIDTH:] = _head_rms(conv, cg_ref[...], bd)
    mix = _dot(_bf16(mix_ref[...]), wout_ref[...]) + bout_ref[...]
    x1_ref[0] = x + gt_ref[0] * mix


def _mixer(x, shift1, scale1, gate1, norm1_g, w_in_r, b_in_r, sinks, conv_w,
           attn_out_g, conv_out_g, w_out_b, b_out):
    bsz, s_len, d = x.shape
    tm = min(TOKEN_TILE, s_len)
    bd = np.kron(np.eye(ATTN_WIDTH // HEAD_DIM, dtype=np.float32),
                 np.full((HEAD_DIM, HEAD_DIM), 1.0 / HEAD_DIM, np.float32))
    bd = jnp.asarray(bd, jnp.bfloat16)
    vec = lambda n: pl.BlockSpec((1, n), lambda b, j: (0, 0))
    per_b = pl.BlockSpec((1, 1, d), lambda b, j: (b, 0, 0))
    full = lambda shp: pl.BlockSpec(shp, lambda b, j: (0,) * len(shp))
    return pl.pallas_call(
        _mixer_kernel,
        grid=(bsz, s_len // tm),
        in_specs=[
            pl.BlockSpec((1, tm, d), lambda b, j: (b, j, 0)),
            per_b, per_b, per_b,
            vec(d),
            full((d, P_W)), vec(P_W),
            pl.BlockSpec(memory_space=pltpu.SMEM),
            full((3, CONV_WIDTH)),
            vec(ATTN_WIDTH), vec(CONV_WIDTH),
            full((d, d)), vec(d),
            full((ATTN_WIDTH, ATTN_WIDTH)),
        ],
        out_specs=pl.BlockSpec((1, tm, d), lambda b, j: (b, j, 0)),
        out_shape=jax.ShapeDtypeStruct((bsz, s_len, d), jnp.float32),
        scratch_shapes=[
            pltpu.VMEM((tm, P_W), jnp.float32),
            pltpu.VMEM((tm, d), jnp.float32),
            pltpu.VMEM((WINDOW, 2 * LANES), jnp.float32),
            pltpu.VMEM((WINDOW, 2 * LANES), jnp.float32),
            pltpu.VMEM((SUBLANES, CONV_WIDTH), jnp.float32),
        ],
        compiler_params=pltpu.CompilerParams(
            dimension_semantics=("arbitrary", "arbitrary"),
            vmem_limit_bytes=VMEM_LIMIT_SMALL),
        name="mixer",
    )(x, shift1.reshape(bsz, 1, d), scale1.reshape(bsz, 1, d), gate1.reshape(bsz, 1, d),
      norm1_g.reshape(1, d), w_in_r, b_in_r.reshape(1, P_W), sinks, conv_w,
      attn_out_g.reshape(1, ATTN_WIDTH), conv_out_g.reshape(1, CONV_WIDTH),
      w_out_b, b_out.reshape(1, d), bd)


def _staircase():
    flat = np.zeros((80,), np.int32)
    ok = np.zeros((80,), bool)
    for a in range(8):
        for b in range(8):
            flat[a * 8 + b] = a * PEER_TOPK + b
            ok[a * 8 + b] = (a + 1) * (b + 1) <= PEER_TOPK
    for r in range(8):
        flat[64 + r] = 8 + r
        ok[64 + r] = True
        flat[72 + r] = (8 + r) * PEER_TOPK
        ok[72 + r] = True
    return flat, ok


def _extract_topk(s, ids, big):
    rows16 = lax.broadcasted_iota(jnp.int32, (PEER_TOPK, LANES), 0)
    vals = jnp.zeros((PEER_TOPK, LANES), jnp.float32)
    idxs = jnp.zeros((PEER_TOPK, LANES), jnp.int32)
    for r in range(PEER_TOPK):
        m = jnp.max(s, axis=0, keepdims=True)
        i = jnp.min(jnp.where(s == m, ids, big), axis=0, keepdims=True)
        vals = jnp.where(rows16 == r, m, vals)
        idxs = jnp.where(rows16 == r, i, idxs)
        s = jnp.where(ids == i, -jnp.inf, s)
    return vals, idxs


def _take_rows(table, sel):
    out = jnp.zeros_like(table)
    for a in range(PEER_TOPK):
        out = jnp.where(sel == a, table[a:a + 1, :], out)
    return out


def _router_kernel(x1_ref, sh_ref, sc_ref, g2_ref, wq_ref, k1_ref, k2_ref,
                   flat_ref, ok_ref, h2_ref, idx_ref, gate_ref,
                   st_ref, oi_ref, og_ref):
    tm = x1_ref.shape[1]
    nt = tm // LANES
    x1 = x1_ref[0]
    h2 = _rms(x1) * g2_ref[...] * (1.0 + sc_ref[0]) + sh_ref[0]
    h2_ref[0] = h2
    q = _dot(_bf16(h2), wq_ref[...])
    for hd in range(PEER_HEADS):
        for side, kref in enumerate((k1_ref, k2_ref)):
            c0 = hd * D_KEY + side * D_HALF
            st = _dot_nt(kref[hd], _bf16(q[:, c0:c0 + D_HALF]))
            for tt in range(nt):
                st_ref[2 * hd + side, tt] = st[:, tt * LANES:(tt + 1) * LANES]

    key_ids = lax.broadcasted_iota(jnp.int32, (N_KEYS, LANES), 0)
    flat = flat_ref[...]
    cand_ok = ok_ref[...] > 0

    def select(hd, tt):
        v1, i1 = _extract_topk(st_ref[2 * hd, tt], key_ids, N_KEYS)
        v2, i2 = _extract_topk(st_ref[2 * hd + 1, tt], key_ids, N_KEYS)
        parts = [v1[a:a + 1, :] + v2[0:SUBLANES, :] for a in range(SUBLANES)]
        parts.append(v1[0:1, :] + v2[SUBLANES:, :])
        parts.append(v1[SUBLANES:, :] + v2[0:1, :])
        cand = jnp.where(cand_ok, jnp.concatenate(parts, axis=0), -jnp.inf)
        sc, cf = _extract_topk(cand, flat, PEER_TOPK * PEER_TOPK)
        e1 = _take_rows(i1, cf >> 4)
        e2 = _take_rows(i2, cf & (PEER_TOPK - 1))
        e = jnp.exp(sc - sc[0:1, :])
        g = e * (1.0 / jnp.sum(e, axis=0, keepdims=True))
        r0 = pl.multiple_of(hd * PEER_TOPK, PEER_TOPK)
        oi_ref[tt, pl.ds(r0, PEER_TOPK), :] = (e1 * N_KEYS + e2) * WORD_ROWS
        og_ref[tt, pl.ds(r0, PEER_TOPK), :] = g

    def body(n, carry):
        hd = lax.div(n, nt // ROUTER_UNROLL)
        t0 = lax.rem(n, nt // ROUTER_UNROLL) * ROUTER_UNROLL
        for k in range(ROUTER_UNROLL):
            select(hd, t0 + k)
        return carry

    lax.fori_loop(0, PEER_HEADS * nt // ROUTER_UNROLL, body, 0)
    for tt in range(nt):
        idx_ref[tt * LANES:(tt + 1) * LANES, :] = oi_ref[tt].T
        gate_ref[tt * LANES:(tt + 1) * LANES, :] = og_ref[tt].T


def _router(x1, shift2, scale2, norm2_g, w_query_b, keys1_b, keys2_b):
    bsz, s_len, d = x1.shape
    tm = min(TOKEN_TILE, s_len)
    nt = tm // LANES
    nj = s_len // tm
    flat, ok = _staircase()
    flat = jnp.asarray(np.broadcast_to(flat[:, None], (80, LANES)), jnp.int32)
    ok = jnp.asarray(np.broadcast_to(ok[:, None], (80, LANES)), jnp.int32)
    per_b = pl.BlockSpec((1, 1, d), lambda b, j: (b, 0, 0))
    full = lambda shp: pl.BlockSpec(shp, lambda b, j: (0,) * len(shp))
    tok = lambda b, j: (b * nj + j, 0)
    return pl.pallas_call(
        _router_kernel,
        grid=(bsz, nj),
        in_specs=[
            pl.BlockSpec((1, tm, d), lambda b, j: (b, j, 0)),
            per_b, per_b,
            full((1, d)),
            full((d, PEER_HEADS * D_KEY)),
            full((PEER_HEADS, N_KEYS, D_HALF)),
            full((PEER_HEADS, N_KEYS, D_HALF)),
            full((80, LANES)), full((80, LANES)),
        ],
        out_specs=[
            pl.BlockSpec((1, tm, d), lambda b, j: (b, j, 0)),
            pl.BlockSpec((tm, N_SEL), tok),
            pl.BlockSpec((tm, N_SEL), tok),
        ],
        out_shape=[
            jax.ShapeDtypeStruct((bsz, s_len, d), jnp.float32),
            jax.ShapeDtypeStruct((bsz * s_len, N_SEL), jnp.int32),
            jax.ShapeDtypeStruct((bsz * s_len, N_SEL), jnp.float32),
        ],
        scratch_shapes=[
            pltpu.VMEM((2 * PEER_HEADS, nt, N_KEYS, LANES), jnp.float32),
            pltpu.VMEM((nt, N_SEL, LANES), jnp.int32),
            pltpu.VMEM((nt, N_SEL, LANES), jnp.float32),
        ],
        compiler_params=pltpu.CompilerParams(
            dimension_semantics=("arbitrary", "arbitrary"),
            vmem_limit_bytes=VMEM_LIMIT_SMALL),
        name="router",
    )(x1, shift2.reshape(bsz, 1, d), scale2.reshape(bsz, 1, d), norm2_g.reshape(1, d),
      w_query_b, keys1_b, keys2_b, flat, ok)


def _pack_table(t):
    e = t.shape[0]
    pairs = jnp.swapaxes(_bf16(t).reshape(e, WORD_ROWS, 2, LANES), -1, -2)
    return lax.bitcast_convert_type(pairs, jnp.int32).reshape(e * WORD_ROWS, LANES)


def _gather_rows(idx_ref, tok, table_ref, tile_ref):
    for m0 in range(0, N_SEL, SUBLANES):
        ids = idx_ref.at[tok, pl.ds(m0, SUBLANES)]
        for k in range(SUBLANES):
            mi = m0 + k
            row = pl.multiple_of(ids[k], WORD_ROWS)
            tile_ref[mi * WORD_ROWS:(mi + 1) * WORD_ROWS, :] = table_ref[pl.ds(row, WORD_ROWS), :]


def _chunk_mask():
    r = lax.broadcasted_iota(jnp.int32, (CHUNKS, N_SEL * CHUNKS), 0)
    c = lax.broadcasted_iota(jnp.int32, (CHUNKS, N_SEL * CHUNKS), 1)
    return (c % CHUNKS) == r


def _token_pipeline(idx_ref, table_ref, tiles, n_tokens, consume):
    g = len(tiles) // 2
    group_a, group_b = tiles[:g], tiles[g:]

    def gather_group(t, group):
        for k in range(g):
            tk = jnp.minimum(t + k, n_tokens - 1)
            _gather_rows(idx_ref, tk, table_ref, group[k])

    def consume_group(t, group):
        for k in range(g):
            consume(t + k, group[k])

    gather_group(0, group_a)
    always = idx_ref[0, 0] >= 0

    def body(p, carry):
        t0 = 2 * g * p

        @pl.when(always)
        def _():
            consume_group(t0, group_a)
            gather_group(t0 + g, group_b)

        @pl.when(always)
        def _():
            consume_group(t0 + g, group_b)
            gather_group(t0 + 2 * g, group_a)

        return carry

    lax.fori_loop(0, n_tokens // (2 * g), body, 0)


def _load_table(table_hbm, table_ref, sem):
    @pl.when(pl.program_id(0) == 0)
    def _():
        cp = pltpu.make_async_copy(table_hbm, table_ref, sem)
        cp.start()
        cp.wait()


def _peer_u_kernel(idx_ref, h_ref, gate_ref, table_hbm, gsum_ref, w_ref,
                   table_ref, *scratch):
    tiles, (d_ref, sem) = scratch[:PEER_TILES], scratch[PEER_TILES:]
    tb = h_ref.shape[0]
    _load_table(table_hbm, table_ref, sem)
    mask = _chunk_mask()

    def scores(t, tile_ref):
        hrow = h_ref[pl.ds(t, 1), :]
        h8 = jnp.concatenate([hrow[:, c * LANES:(c + 1) * LANES] for c in range(CHUNKS)], axis=0)
        hi, lo = _split2(h8)
        lhs = jnp.concatenate([hi, lo], axis=0)
        xt = pltpu.bitcast(tile_ref[...], jnp.bfloat16)
        y = _dot_nt(lhs, xt)
        yc = y[0:CHUNKS] + y[CHUNKS:]
        d_ref[pl.ds(t, 1), :] = jnp.sum(jnp.where(mask, yc, 0.0), axis=0, keepdims=True)

    _token_pipeline(idx_ref, table_ref, tiles, tb, scores)
    s = _dot_sel(d_ref[...], gsum_ref[...])
    act = 0.5 * s * (1.0 + lax.erf(s * (2.0 ** -0.5)))
    w_ref[...] = gate_ref[...] * act


def _peer_v_kernel(idx_ref, w_ref, table_hbm, expand_ref, o_ref,
                   table_ref, *scratch):
    tiles, (whi_ref, wlo_ref, sem) = scratch[:PEER_TILES], scratch[PEER_TILES:]
    tb = w_ref.shape[0]
    _load_table(table_hbm, table_ref, sem)

    w_hi, w_lo = _split2(w_ref[...])
    whi_ref[...] = _dot(w_hi, expand_ref[...])
    wlo_ref[...] = _dot(w_lo, expand_ref[...])
    mask = _chunk_mask()

    def combine(t, tile_ref):
        hi = jnp.broadcast_to(whi_ref[pl.ds(t, 1), :], mask.shape)
        lo = jnp.broadcast_to(wlo_ref[pl.ds(t, 1), :], mask.shape)
        lhs = jnp.concatenate([_bf16(jnp.where(mask, hi, 0.0)),
                               _bf16(jnp.where(mask, lo, 0.0))], axis=0)
        xt = pltpu.bitcast(tile_ref[...], jnp.bfloat16)
        o = _dot(lhs, xt)
        o8 = o[0:CHUNKS] + o[CHUNKS:]
        o_ref[pl.ds(t, 1), :] = jnp.concatenate([o8[c:c + 1, :] for c in range(CHUNKS)], axis=1)

    _token_pipeline(idx_ref, table_ref, tiles, tb, combine)


def _peer_scratch(tb):
    tile = pltpu.VMEM((N_SEL * WORD_ROWS, LANES), jnp.int32)
    return [pltpu.VMEM((N_EXPERTS * WORD_ROWS, LANES), jnp.int32)] + [tile] * PEER_TILES


def _peer_u(idx, h2, gates, table):
    t = h2.shape[0]
    tb = min(PEER_TILE, t)
    gsum = np.kron(np.eye(N_SEL, dtype=np.float32), np.ones((CHUNKS, 1), np.float32))
    gsum = jnp.asarray(gsum, jnp.bfloat16)
    return pl.pallas_call(
        _peer_u_kernel,
        grid=(t // tb,),
        in_specs=[
            pl.BlockSpec((tb, N_SEL), lambda i: (i, 0), memory_space=pltpu.SMEM),
            pl.BlockSpec((tb, N_SEL * CHUNKS), lambda i: (i, 0)),
            pl.BlockSpec((tb, N_SEL), lambda i: (i, 0)),
            pl.BlockSpec(memory_space=pl.ANY),
            pl.BlockSpec((N_SEL * CHUNKS, N_SEL), lambda i: (0, 0)),
        ],
        out_specs=pl.BlockSpec((tb, N_SEL), lambda i: (i, 0)),
        out_shape=jax.ShapeDtypeStruct((t, N_SEL), jnp.float32),
        scratch_shapes=_peer_scratch(tb) + [
            pltpu.VMEM((tb, N_SEL * CHUNKS), jnp.float32),
            pltpu.SemaphoreType.DMA,
        ],
        compiler_params=pltpu.CompilerParams(
            dimension_semantics=("arbitrary",),
            vmem_limit_bytes=VMEM_LIMIT_TABLE),
        name="peer_u",
    )(idx, h2, gates, table, gsum)


def _peer_v(idx, w, table):
    t = w.shape[0]
    tb = min(PEER_TILE, t)
    expand = np.kron(np.eye(N_SEL, dtype=np.float32), np.ones((1, CHUNKS), np.float32))
    expand = jnp.asarray(expand, jnp.bfloat16)
    return pl.pallas_call(
        _peer_v_kernel,
        grid=(t // tb,),
        in_specs=[
            pl.BlockSpec((tb, N_SEL), lambda i: (i, 0), memory_space=pltpu.SMEM),
            pl.BlockSpec((tb, N_SEL), lambda i: (i, 0)),
            pl.BlockSpec(memory_space=pl.ANY),
            pl.BlockSpec((N_SEL, N_SEL * CHUNKS), lambda i: (0, 0)),
        ],
        out_specs=pl.BlockSpec((tb, N_SEL * CHUNKS), lambda i: (i, 0)),
        out_shape=jax.ShapeDtypeStruct((t, N_SEL * CHUNKS), jnp.float32),
        scratch_shapes=_peer_scratch(tb) + [
            pltpu.VMEM((tb, N_SEL * CHUNKS), jnp.float32),
            pltpu.VMEM((tb, N_SEL * CHUNKS), jnp.float32),
            pltpu.SemaphoreType.DMA,
        ],
        compiler_params=pltpu.CompilerParams(
            dimension_semantics=("arbitrary",),
            vmem_limit_bytes=VMEM_LIMIT_TABLE),
        name="peer_v",
    )(idx, w, table, expand)


def _final_kernel(x1_ref, p_ref, gt_ref, g_ref, y_ref):
    x2 = x1_ref[0] + gt_ref[0] * p_ref[0]
    y_ref[0] = _rms(x2) * g_ref[...]


def _final(x1, peer, gate2, final_g):
    bsz, s_len, d = x1.shape
    tm = min(TOKEN_TILE, s_len)
    blk = pl.BlockSpec((1, tm, d), lambda b, j: (b, j, 0))
    return pl.pallas_call(
        _final_kernel,
        grid=(bsz, s_len // tm),
        in_specs=[blk, blk,
                  pl.BlockSpec((1, 1, d), lambda b, j: (b, 0, 0)),
                  pl.BlockSpec((1, d), lambda b, j: (0, 0))],
        out_specs=blk,
        out_shape=jax.ShapeDtypeStruct((bsz, s_len, d), jnp.float32),
        compiler_params=pltpu.CompilerParams(
            dimension_semantics=("arbitrary", "arbitrary"),
            vmem_limit_bytes=VMEM_LIMIT_SMALL),
        name="final",
    )(x1, peer, gate2.reshape(bsz, 1, d), final_g.reshape(1, d))


def _relayout_in_proj(w_in, b_in):
    def cols(a):
        q = a[..., 0:512]
        k = a[..., 512:640]
        v = a[..., 640:768]
        rest = a[..., 768:]
        dup = lambda t: jnp.concatenate(
            [t[..., 0:64], t[..., 0:64], t[..., 64:128], t[..., 64:128]], axis=-1)
        return jnp.concatenate([q, dup(k), dup(v), rest], axis=-1)
    return cols(w_in), cols(b_in)


def kernel(x, c, w_ada, b_ada, norm1_g, w_in, b_in, attn_sinks, conv_w, attn_out_g,
           conv_out_g, w_out, b_out, norm2_g, w_query, peer_keys1, peer_keys2,
           peer_u, peer_v, final_g):
    bsz, s_len, d = x.shape
    ada = _ada(c, w_ada, b_ada)
    shift1, scale1, gate1, shift2, scale2, gate2 = jnp.split(ada, 6, axis=-1)

    w_in_r, b_in_r = _relayout_in_proj(w_in, b_in)
    x1 = _mixer(x, shift1, scale1, gate1, norm1_g, _bf16(w_in_r), b_in_r, attn_sinks,
                conv_w, attn_out_g, conv_out_g, _bf16(w_out), b_out)

    h2, idx, gates = _router(x1, shift2, scale2, norm2_g, _bf16(w_query),
                             _bf16(peer_keys1), _bf16(peer_keys2))
    t = bsz * s_len
    w = _peer_u(idx, h2.reshape(t, d), gates, _pack_table(peer_u))
    peer = _peer_v(idx, w, _pack_table(peer_v))
    return _final(x1, peer.reshape(bsz, s_len, d), gate2, final_g)
```

```python
import functools

import jax
import jax.numpy as jnp
import numpy as np
from jax import lax
from jax.experimental import pallas as pl
from jax.experimental.pallas import tpu as pltpu

LANES = 128
SUBLANES = 8

D_MODEL = 1024
HEAD_DIM = 64
ATTN_WIDTH = 512
CONV_WIDTH = 512
N_HEADS = 8
N_KV_HEADS = 2
WINDOW = 128
N_KEYS = 128
N_EXPERTS = N_KEYS * N_KEYS
PEER_HEADS = 8
PEER_TOPK = 16
D_KEY = 256
D_HALF = 128
N_SEL = PEER_HEADS * PEER_TOPK
EPS = 1e-6
NEG = -1e30

P_Q = 0
P_K = 512
P_V = 768
P_B = 1024
P_C = 1536
P_U = 2048
P_W = 2560

CHUNKS = D_MODEL // LANES
WORD_ROWS = CHUNKS // 2

TOKEN_TILE = 512
ROUTER_UNROLL = 4
PEER_TILE = 256
PEER_TILES = 16

VMEM_LIMIT_SMALL = 48 * 1024 * 1024
VMEM_LIMIT_TABLE = 56 * 1024 * 1024


def _bf16(x):
    return x.astype(jnp.bfloat16)


def _dot(a, b):
    return jnp.dot(a, b, preferred_element_type=jnp.float32)


def _dot_nt(a, b):
    return lax.dot_general(a, b, (((1,), (1,)), ((), ())),
                           preferred_element_type=jnp.float32)


def _split2(x):
    hi = _bf16(x)
    lo = _bf16(x - hi.astype(jnp.float32))
    return hi, lo


def _split3(x):
    hi = _bf16(x)
    r = x - hi.astype(jnp.float32)
    mid = _bf16(r)
    lo = _bf16(r - mid.astype(jnp.float32))
    return hi, mid, lo


def _dot_sel(x, sel):
    hi, mid, lo = _split3(x)
    return _dot(hi, sel) + _dot(mid, sel) + _dot(lo, sel)


def _rms(x):
    return x * lax.rsqrt(jnp.mean(x * x, axis=-1, keepdims=True) + EPS)


def _ada_kernel(c_ref, w_ref, b_ref, o_ref):
    c = c_ref[...]
    a = c * (1.0 / (1.0 + jnp.exp(-c)))
    hi, mid, lo = _split3(a)
    w = w_ref[...]
    w_hi, w_mid, w_lo = _split3(w)
    acc = _dot(hi, w_hi)
    acc += _dot(hi, w_mid) + _dot(mid, w_hi)
    acc += _dot(hi, w_lo) + _dot(mid, w_mid) + _dot(lo, w_hi)
    o_ref[...] = acc + b_ref[...]


def _ada(c, w_ada, b_ada):
    bsz, d = c.shape
    n = w_ada.shape[1]
    blk = d
    return pl.pallas_call(
        _ada_kernel,
        grid=(n // blk,),
        in_specs=[
            pl.BlockSpec((bsz, d), lambda i: (0, 0)),
            pl.BlockSpec((d, blk), lambda i: (0, i)),
            pl.BlockSpec((1, blk), lambda i: (0, i)),
        ],
        out_specs=pl.BlockSpec((bsz, blk), lambda i: (0, i)),
        out_shape=jax.ShapeDtypeStruct((bsz, n), jnp.float32),
        compiler_params=pltpu.CompilerParams(
            dimension_semantics=("arbitrary",),
            vmem_limit_bytes=VMEM_LIMIT_SMALL),
        name="ada",
    )(c, w_ada, b_ada.reshape(1, n))


def _head_rms(o, gain, bd):
    hi, lo = _split2(o * o)
    ms = _dot(hi, bd) + _dot(lo, bd)
    return o * lax.rsqrt(ms + EPS) * gain


def _mixer_kernel(x_ref, sh_ref, sc_ref, gt_ref, g1_ref, win_ref, bin_ref,
                  sink_ref, cw_ref, ag_ref, cg_ref, wout_ref, bout_ref, bd_ref,
                  x1_ref, proj_ref, mix_ref, kc_ref, vc_ref, zc_ref):
    j = pl.program_id(1)
    tm = x_ref.shape[1]

    @pl.when(j == 0)
    def _():
        kc_ref[...] = jnp.zeros_like(kc_ref)
        vc_ref[...] = jnp.zeros_like(vc_ref)
        zc_ref[...] = jnp.zeros_like(zc_ref)

    x = x_ref[0]
    h = _rms(x) * g1_ref[...] * (1.0 + sc_ref[0]) + sh_ref[0]
    proj_ref[...] = _dot(_bf16(h), win_ref[...]) + bin_ref[...]

    qi = lax.broadcasted_iota(jnp.int32, (WINDOW, 2 * WINDOW), 0)
    kj = lax.broadcasted_iota(jnp.int32, (WINDOW, 2 * WINDOW), 1)
    valid = (kj > qi) & (kj <= qi + WINDOW)
    valid_first = valid & (kj >= jnp.where(j > 0, 0, WINDOW))
    lane_lo = lax.broadcasted_iota(jnp.int32, (1, LANES), 1) < HEAD_DIM
    scale = HEAD_DIM ** -0.5

    k_prev = _bf16(kc_ref[...])
    v_prev = _bf16(vc_ref[...])
    for n in range(tm // WINDOW):
        r0 = n * WINDOW
        k_cur = _bf16(proj_ref[r0:r0 + WINDOW, P_K:P_K + 2 * LANES])
        v_cur = _bf16(proj_ref[r0:r0 + WINDOW, P_V:P_V + 2 * LANES])
        kk = jnp.concatenate([k_prev, k_cur], axis=0)
        vv = jnp.concatenate([v_prev, v_cur], axis=0)
        ok = valid_first if n == 0 else valid
        for kvh in range(N_KV_HEADS):
            kt = kk[:, kvh * LANES:(kvh + 1) * LANES]
            vt = vv[:, kvh * LANES:(kvh + 1) * LANES]
            zero = jnp.zeros_like(kt)
            k_half = (jnp.where(lane_lo, kt, zero), jnp.where(lane_lo, zero, kt))
            v_half = (jnp.where(lane_lo, vt, zero), jnp.where(lane_lo, zero, vt))
            for a in (2 * kvh, 2 * kvh + 1):
                qt = _bf16(proj_ref[r0:r0 + WINDOW, P_Q + a * LANES:P_Q + (a + 1) * LANES])
                o_tile = None
                for par in range(2):
                    sink = sink_ref[2 * a + par]
                    s = _dot_nt(qt, k_half[par]) * scale
                    s = jnp.where(ok, s, NEG)
                    m = jnp.maximum(jnp.max(s, axis=-1, keepdims=True), sink)
                    e = jnp.exp(s - m)
                    den = jnp.sum(e, axis=-1, keepdims=True) + jnp.exp(sink - m)
                    p = _bf16(e * (1.0 / den))
                    o = _dot(p, v_half[par])
                    o_tile = o if o_tile is None else o_tile + o
                mix_ref[r0:r0 + WINDOW, a * LANES:(a + 1) * LANES] = o_tile
        k_prev, v_prev = k_cur, v_cur
    kc_ref[...] = k_prev.astype(jnp.float32)
    vc_ref[...] = v_prev.astype(jnp.float32)

    z = proj_ref[:, P_C:P_C + CONV_WIDTH] * proj_ref[:, P_U:P_U + CONV_WIDTH]
    zc = zc_ref[...]
    row8 = lax.broadcasted_iota(jnp.int32, (SUBLANES, CONV_WIDTH), 0)
    z1 = pltpu.roll(z, 1, axis=0)
    z2 = pltpu.roll(z, 2, axis=0)
    z1_top = jnp.where(row8 < 1, pltpu.roll(zc, 1, axis=0), z1[0:SUBLANES])
    z2_top = jnp.where(row8 < 2, pltpu.roll(zc, 2, axis=0), z2[0:SUBLANES])
    z1 = jnp.concatenate([z1_top, z1[SUBLANES:]], axis=0)
    z2 = jnp.concatenate([z2_top, z2[SUBLANES:]], axis=0)
    zc_ref[...] = z[tm - SUBLANES:tm]
    conv = z2 * cw_ref[0:1, :] + z1 * cw_ref[1:2, :] + z * cw_ref[2:3, :]
    conv = proj_ref[:, P_B:P_B + CONV_WIDTH] * conv

    bd = bd_ref[...]
    mix_ref[:, 0:ATTN_WIDTH] = _head_rms(mix_ref[:, 0:ATTN_WIDTH], ag_ref[...], bd)
    mix_ref[:, ATTN_WIDTH:] = _head_rms(conv, cg_ref[...], bd)
    mix = _dot(_bf16(mix_ref[...]), wout_ref[...]) + bout_ref[...]
    x1_ref[0] = x + gt_ref[0] * mix


def _mixer(x, shift1, scale1, gate1, norm1_g, w_in_r, b_in_r, sinks, conv_w,
           attn_out_g, conv_out_g, w_out_b, b_out):
    bsz, s_len, d = x.shape
    tm = min(TOKEN_TILE, s_len)
    bd = np.kron(np.eye(ATTN_WIDTH // HEAD_DIM, dtype=np.float32),
                 np.full((HEAD_DIM, HEAD_DIM), 1.0 / HEAD_DIM, np.float32))
    bd = jnp.asarray(bd, jnp.bfloat16)
    vec = lambda n: pl.BlockSpec((1, n), lambda b, j: (0, 0))
    per_b = pl.BlockSpec((1, 1, d), lambda b, j: (b, 0, 0))
    full = lambda shp: pl.BlockSpec(shp, lambda b, j: (0,) * len(shp))
    return pl.pallas_call(
        _mixer_kernel,
        grid=(bsz, s_len // tm),
        in_specs=[
            pl.BlockSpec((1, tm, d), lambda b, j: (b, j, 0)),
            per_b, per_b, per_b,
            vec(d),
            full((d, P_W)), vec(P_W),
            pl.BlockSpec(memory_space=pltpu.SMEM),
            full((3, CONV_WIDTH)),
            vec(ATTN_WIDTH), vec(CONV_WIDTH),
            full((d, d)), vec(d),
            full((ATTN_WIDTH, ATTN_WIDTH)),
        ],
        out_specs=pl.BlockSpec((1, tm, d), lambda b, j: (b, j, 0)),
        out_shape=jax.ShapeDtypeStruct((bsz, s_len, d), jnp.float32),
        scratch_shapes=[
            pltpu.VMEM((tm, P_W), jnp.float32),
            pltpu.VMEM((tm, d), jnp.float32),
            pltpu.VMEM((WINDOW, 2 * LANES), jnp.float32),
            pltpu.VMEM((WINDOW, 2 * LANES), jnp.float32),
            pltpu.VMEM((SUBLANES, CONV_WIDTH), jnp.float32),
        ],
        compiler_params=pltpu.CompilerParams(
            dimension_semantics=("arbitrary", "arbitrary"),
            vmem_limit_bytes=VMEM_LIMIT_SMALL),
        name="mixer",
    )(x, shift1.reshape(bsz, 1, d), scale1.reshape(bsz, 1, d), gate1.reshape(bsz, 1, d),
      norm1_g.reshape(1, d), w_in_r, b_in_r.reshape(1, P_W), sinks, conv_w,
      attn_out_g.reshape(1, ATTN_WIDTH), conv_out_g.reshape(1, CONV_WIDTH),
      w_out_b, b_out.reshape(1, d), bd)


def _staircase():
    flat = np.zeros((80,), np.int32)
    ok = np.zeros((80,), bool)
    for a in range(8):
        for b in range(8):
            flat[a * 8 + b] = a * PEER_TOPK + b
            ok[a * 8 + b] = (a + 1) * (b + 1) <= PEER_TOPK
    for r in range(8):
        flat[64 + r] = 8 + r
        ok[64 + r] = True
        flat[72 + r] = (8 + r) * PEER_TOPK
        ok[72 + r] = True
    return flat, ok


def _extract_topk(s, ids, big):
    vals, idxs = [], []
    for _ in range(PEER_TOPK):
        m = jnp.max(s, axis=0, keepdims=True)
        i = jnp.min(jnp.where(s == m, ids, big), axis=0, keepdims=True)
        vals.append(m)
        idxs.append(i)
        s = jnp.where(ids == i, -jnp.inf, s)
    return jnp.concatenate(vals, axis=0), jnp.concatenate(idxs, axis=0)


def _take_rows(table, sel):
    out = jnp.zeros_like(table)
    for a in range(PEER_TOPK):
        out = jnp.where(sel == a, table[a:a + 1, :], out)
    return out


def _router_kernel(x1_ref, sh_ref, sc_ref, g2_ref, wq_ref, k1_ref, k2_ref,
                   flat_ref, ok_ref, h2_ref, idx_ref, gate_ref,
                   st_ref, oi_ref, og_ref):
    tm = x1_ref.shape[1]
    nt = tm // LANES
    x1 = x1_ref[0]
    h2 = _rms(x1) * g2_ref[...] * (1.0 + sc_ref[0]) + sh_ref[0]
    h2_ref[0] = h2
    q = _dot(_bf16(h2), wq_ref[...])
    for hd in range(PEER_HEADS):
        for side, kref in enumerate((k1_ref, k2_ref)):
            c0 = hd * D_KEY + side * D_HALF
            st = _dot_nt(kref[hd], _bf16(q[:, c0:c0 + D_HALF]))
            for tt in range(nt):
                st_ref[2 * hd + side, tt] = st[:, tt * LANES:(tt + 1) * LANES]

    key_ids = lax.broadcasted_iota(jnp.int32, (N_KEYS, LANES), 0).astype(jnp.float32)
    flat = flat_ref[...]
    cand_ok = ok_ref[...] > 0

    def select(hd, tt):
        v1, i1 = _extract_topk(st_ref[2 * hd, tt], key_ids, float(N_KEYS))
        v2, i2 = _extract_topk(st_ref[2 * hd + 1, tt], key_ids, float(N_KEYS))
        parts = [v1[a:a + 1, :] + v2[0:SUBLANES, :] for a in range(SUBLANES)]
        parts.append(v1[0:1, :] + v2[SUBLANES:, :])
        parts.append(v1[SUBLANES:, :] + v2[0:1, :])
        cand = jnp.where(cand_ok, jnp.concatenate(parts, axis=0), -jnp.inf)
        sc, cf = _extract_topk(cand, flat, float(PEER_TOPK * PEER_TOPK))
        cf = cf.astype(jnp.int32)
        e1 = _take_rows(i1, cf >> 4).astype(jnp.int32)
        e2 = _take_rows(i2, cf & (PEER_TOPK - 1)).astype(jnp.int32)
        e = jnp.exp(sc - sc[0:1, :])
        g = e * (1.0 / jnp.sum(e, axis=0, keepdims=True))
        r0 = pl.multiple_of(hd * PEER_TOPK, PEER_TOPK)
        oi_ref[tt, pl.ds(r0, PEER_TOPK), :] = (e1 * N_KEYS + e2) * WORD_ROWS
        og_ref[tt, pl.ds(r0, PEER_TOPK), :] = g

    def body(n, carry):
        hd = lax.div(n, nt // ROUTER_UNROLL)
        t0 = lax.rem(n, nt // ROUTER_UNROLL) * ROUTER_UNROLL
        for k in range(ROUTER_UNROLL):
            select(hd, t0 + k)
        return carry

    lax.fori_loop(0, PEER_HEADS * nt // ROUTER_UNROLL, body, 0)
    for tt in range(nt):
        idx_ref[tt * LANES:(tt + 1) * LANES, :] = oi_ref[tt].T
        gate_ref[tt * LANES:(tt + 1) * LANES, :] = og_ref[tt].T


def _router(x1, shift2, scale2, norm2_g, w_query_b, keys1_b, keys2_b):
    bsz, s_len, d = x1.shape
    tm = min(TOKEN_TILE, s_len)
    nt = tm // LANES
    nj = s_len // tm
    flat, ok = _staircase()
    flat = jnp.asarray(np.broadcast_to(flat[:, None], (80, LANES)), jnp.float32)
    ok = jnp.asarray(np.broadcast_to(ok[:, None], (80, LANES)), jnp.int32)
    per_b = pl.BlockSpec((1, 1, d), lambda b, j: (b, 0, 0))
    full = lambda shp: pl.BlockSpec(shp, lambda b, j: (0,) * len(shp))
    tok = lambda b, j: (b * nj + j, 0)
    return pl.pallas_call(
        _router_kernel,
        grid=(bsz, nj),
        in_specs=[
            pl.BlockSpec((1, tm, d), lambda b, j: (b, j, 0)),
            per_b, per_b,
            full((1, d)),
            full((d, PEER_HEADS * D_KEY)),
            full((PEER_HEADS, N_KEYS, D_HALF)),
            full((PEER_HEADS, N_KEYS, D_HALF)),
            full((80, LANES)), full((80, LANES)),
        ],
        out_specs=[
            pl.BlockSpec((1, tm, d), lambda b, j: (b, j, 0)),
            pl.BlockSpec((tm, N_SEL), tok),
            pl.BlockSpec((tm, N_SEL), tok),
        ],
        out_shape=[
            jax.ShapeDtypeStruct((bsz, s_len, d), jnp.float32),
            jax.ShapeDtypeStruct((bsz * s_len, N_SEL), jnp.int32),
            jax.ShapeDtypeStruct((bsz * s_len, N_SEL), jnp.float32),
        ],
        scratch_shapes=[
            pltpu.VMEM((2 * PEER_HEADS, nt, N_KEYS, LANES), jnp.float32),
            pltpu.VMEM((nt, N_SEL, LANES), jnp.int32),
            pltpu.VMEM((nt, N_SEL, LANES), jnp.float32),
        ],
        compiler_params=pltpu.CompilerParams(
            dimension_semantics=("arbitrary", "arbitrary"),
            vmem_limit_bytes=VMEM_LIMIT_SMALL),
        name="router",
    )(x1, shift2.reshape(bsz, 1, d), scale2.reshape(bsz, 1, d), norm2_g.reshape(1, d),
      w_query_b, keys1_b, keys2_b, flat, ok)


def _pack_kernel(t_ref, o_ref):
    o_ref[...] = pltpu.bitcast(_bf16(t_ref[...]), jnp.int32)


def _pack_table(t):
    rows = t.shape[0] * CHUNKS
    blk = 4096
    return pl.pallas_call(
        _pack_kernel,
        grid=(rows // blk,),
        in_specs=[pl.BlockSpec((blk, LANES), lambda i: (i, 0))],
        out_specs=pl.BlockSpec((blk // 2, LANES), lambda i: (i, 0)),
        out_shape=jax.ShapeDtypeStruct((rows // 2, LANES), jnp.int32),
        compiler_params=pltpu.CompilerParams(
            dimension_semantics=("arbitrary",),
            vmem_limit_bytes=VMEM_LIMIT_SMALL),
        name="pack_table",
    )(t.reshape(rows, LANES))


def _gather_rows(idx_ref, tok, table_ref, tile_ref):
    for m0 in range(0, N_SEL, SUBLANES):
        ids = idx_ref.at[tok, pl.ds(m0, SUBLANES)]
        for k in range(SUBLANES):
            mi = m0 + k
            row = pl.multiple_of(ids[k], WORD_ROWS)
            tile_ref[mi * WORD_ROWS:(mi + 1) * WORD_ROWS, :] = table_ref[pl.ds(row, WORD_ROWS), :]


def _chunk_mask():
    r = lax.broadcasted_iota(jnp.int32, (CHUNKS, N_SEL * CHUNKS), 0)
    c = lax.broadcasted_iota(jnp.int32, (CHUNKS, N_SEL * CHUNKS), 1)
    return (c % CHUNKS) == r


def _token_pipeline(idx_ref, table_ref, tiles, n_tokens, consume):
    g = len(tiles) // 2
    group_a, group_b = tiles[:g], tiles[g:]

    def gather_group(t, group):
        for k in range(g):
            tk = jnp.minimum(t + k, n_tokens - 1)
            _gather_rows(idx_ref, tk, table_ref, group[k])

    def consume_group(t, group):
        for k in range(g):
            consume(t + k, group[k])

    gather_group(0, group_a)
    always = idx_ref[0, 0] >= 0

    def body(p, carry):
        t0 = 2 * g * p

        @pl.when(always)
        def _():
            consume_group(t0, group_a)
            gather_group(t0 + g, group_b)

        @pl.when(always)
        def _():
            consume_group(t0 + g, group_b)
            gather_group(t0 + 2 * g, group_a)

        return carry

    lax.fori_loop(0, n_tokens // (2 * g), body, 0)


def _load_table(table_hbm, table_ref, sem):
    @pl.when(pl.program_id(0) == 0)
    def _():
        cp = pltpu.make_async_copy(table_hbm, table_ref, sem)
        cp.start()
        cp.wait()


def _peer_u_kernel(idx_ref, h_ref, gate_ref, table_hbm, gsum_ref, w_ref,
                   table_ref, *scratch):
    tiles, (d_ref, sem) = scratch[:PEER_TILES], scratch[PEER_TILES:]
    tb = h_ref.shape[0]
    _load_table(table_hbm, table_ref, sem)
    mask = _chunk_mask()

    def scores(t, tile_ref):
        hrow = h_ref[pl.ds(t, 1), :]
        h8 = jnp.concatenate([hrow[:, c * LANES:(c + 1) * LANES] for c in range(CHUNKS)], axis=0)
        hi, lo = _split2(h8)
        lhs = jnp.concatenate([hi, lo], axis=0)
        xt = pltpu.bitcast(tile_ref[...], jnp.bfloat16)
        y = _dot_nt(lhs, xt)
        yc = y[0:CHUNKS] + y[CHUNKS:]
        d_ref[pl.ds(t, 1), :] = jnp.sum(jnp.where(mask, yc, 0.0), axis=0, keepdims=True)

    _token_pipeline(idx_ref, table_ref, tiles, tb, scores)
    s = _dot_sel(d_ref[...], gsum_ref[...])
    act = 0.5 * s * (1.0 + lax.erf(s * (2.0 ** -0.5)))
    w_ref[...] = gate_ref[...] * act


def _peer_v_kernel(idx_ref, w_ref, table_hbm, expand_ref, x1_ref, gt_ref, fg_ref, o_ref,
                   table_ref, *scratch):
    tiles, (whi_ref, wlo_ref, sem) = scratch[:PEER_TILES], scratch[PEER_TILES:]
    tb = w_ref.shape[0]
    _load_table(table_hbm, table_ref, sem)

    w_hi, w_lo = _split2(w_ref[...])
    whi_ref[...] = _dot(w_hi, expand_ref[...])
    wlo_ref[...] = _dot(w_lo, expand_ref[...])
    mask = _chunk_mask()

    def combine(t, tile_ref):
        hi = jnp.broadcast_to(whi_ref[pl.ds(t, 1), :], mask.shape)
        lo = jnp.broadcast_to(wlo_ref[pl.ds(t, 1), :], mask.shape)
        lhs = jnp.concatenate([_bf16(jnp.where(mask, hi, 0.0)),
                               _bf16(jnp.where(mask, lo, 0.0))], axis=0)
        xt = pltpu.bitcast(tile_ref[...], jnp.bfloat16)
        o = _dot(lhs, xt)
        o8 = o[0:CHUNKS] + o[CHUNKS:]
        o_ref[pl.ds(t, 1), :] = jnp.concatenate([o8[c:c + 1, :] for c in range(CHUNKS)], axis=1)

    _token_pipeline(idx_ref, table_ref, tiles, tb, combine)
    x2 = x1_ref[...] + gt_ref[0] * o_ref[...]
    o_ref[...] = _rms(x2) * fg_ref[...]


def _peer_scratch(tb):
    tile = pltpu.VMEM((N_SEL * WORD_ROWS, LANES), jnp.int32)
    return [pltpu.VMEM((N_EXPERTS * WORD_ROWS, LANES), jnp.int32)] + [tile] * PEER_TILES


def _peer_u(idx, h2, gates, table):
    t = h2.shape[0]
    tb = min(PEER_TILE, t)
    gsum = np.kron(np.eye(N_SEL, dtype=np.float32), np.ones((CHUNKS, 1), np.float32))
    gsum = jnp.asarray(gsum, jnp.bfloat16)
    return pl.pallas_call(
        _peer_u_kernel,
        grid=(t // tb,),
        in_specs=[
            pl.BlockSpec((tb, N_SEL), lambda i: (i, 0), memory_space=pltpu.SMEM),
            pl.BlockSpec((tb, N_SEL * CHUNKS), lambda i: (i, 0)),
            pl.BlockSpec((tb, N_SEL), lambda i: (i, 0)),
            pl.BlockSpec(memory_space=pl.ANY),
            pl.BlockSpec((N_SEL * CHUNKS, N_SEL), lambda i: (0, 0)),
        ],
        out_specs=pl.BlockSpec((tb, N_SEL), lambda i: (i, 0)),
        out_shape=jax.ShapeDtypeStruct((t, N_SEL), jnp.float32),
        scratch_shapes=_peer_scratch(tb) + [
            pltpu.VMEM((tb, N_SEL * CHUNKS), jnp.float32),
            pltpu.SemaphoreType.DMA,
        ],
        compiler_params=pltpu.CompilerParams(
            dimension_semantics=("arbitrary",),
            vmem_limit_bytes=VMEM_LIMIT_TABLE),
        name="peer_u",
    )(idx, h2, gates, table, gsum)


def _peer_v(idx, w, table, x1, gate2, final_g):
    bsz, s_len, d = x1.shape
    t = w.shape[0]
    tb = min(PEER_TILE, t, s_len)
    assert s_len % tb == 0
    expand = np.kron(np.eye(N_SEL, dtype=np.float32), np.ones((1, CHUNKS), np.float32))
    expand = jnp.asarray(expand, jnp.bfloat16)
    return pl.pallas_call(
        _peer_v_kernel,
        grid=(t // tb,),
        in_specs=[
            pl.BlockSpec((tb, N_SEL), lambda i: (i, 0), memory_space=pltpu.SMEM),
            pl.BlockSpec((tb, N_SEL), lambda i: (i, 0)),
            pl.BlockSpec(memory_space=pl.ANY),
            pl.BlockSpec((N_SEL, N_SEL * CHUNKS), lambda i: (0, 0)),
            pl.BlockSpec((tb, d), lambda i: (i, 0)),
            pl.BlockSpec((1, 1, d), lambda i: (i * tb // s_len, 0, 0)),
            pl.BlockSpec((1, d), lambda i: (0, 0)),
        ],
        out_specs=pl.BlockSpec((tb, d), lambda i: (i, 0)),
        out_shape=jax.ShapeDtypeStruct((t, d), jnp.float32),
        scratch_shapes=_peer_scratch(tb) + [
            pltpu.VMEM((tb, N_SEL * CHUNKS), jnp.float32),
            pltpu.VMEM((tb, N_SEL * CHUNKS), jnp.float32),
            pltpu.SemaphoreType.DMA,
        ],
        compiler_params=pltpu.CompilerParams(
            dimension_semantics=("arbitrary",),
            vmem_limit_bytes=VMEM_LIMIT_TABLE),
        name="peer_v",
    )(idx, w, table, expand, x1.reshape(t, d), gate2.reshape(bsz, 1, d), final_g.reshape(1, d))


def _relayout_in_proj(w_in, b_in):
    def cols(a):
        q = a[..., 0:512]
        k = a[..., 512:640]
        v = a[..., 640:768]
        rest = a[..., 768:]
        dup = lambda t: jnp.concatenate(
            [t[..., 0:64], t[..., 0:64], t[..., 64:128], t[..., 64:128]], axis=-1)
        return jnp.concatenate([q, dup(k), dup(v), rest], axis=-1)
    return cols(w_in), cols(b_in)


def kernel(x, c, w_ada, b_ada, norm1_g, w_in, b_in, attn_sinks, conv_w, attn_out_g,
           conv_out_g, w_out, b_out, norm2_g, w_query, peer_keys1, peer_keys2,
           peer_u, peer_v, final_g):
    bsz, s_len, d = x.shape
    ada = _ada(c, w_ada, b_ada)
    shift1, scale1, gate1, shift2, scale2, gate2 = jnp.split(ada, 6, axis=-1)

    w_in_r, b_in_r = _relayout_in_proj(w_in, b_in)
    x1 = _mixer(x, shift1, scale1, gate1, norm1_g, _bf16(w_in_r), b_in_r, attn_sinks,
                conv_w, attn_out_g, conv_out_g, _bf16(w_out), b_out)

    h2, idx, gates = _router(x1, shift2, scale2, norm2_g, _bf16(w_query),
                             _bf16(peer_keys1), _bf16(peer_keys2))
    t = bsz * s_len
    w = _peer_u(idx, h2.reshape(t, d), gates, _pack_table(peer_u))
    y = _peer_v(idx, w, _pack_table(peer_v), x1, gate2, final_g)
    return y.reshape(bsz, s_len, d)
```

```python
import functools

import jax
import jax.numpy as jnp
import numpy as np
from jax import lax
from jax.experimental import pallas as pl
from jax.experimental.pallas import tpu as pltpu

LANES = 128
SUBLANES = 8

D_MODEL = 1024
HEAD_DIM = 64
ATTN_WIDTH = 512
CONV_WIDTH = 512
N_HEADS = 8
N_KV_HEADS = 2
WINDOW = 128
N_KEYS = 128
N_EXPERTS = N_KEYS * N_KEYS
PEER_HEADS = 8
PEER_TOPK = 16
D_KEY = 256
D_HALF = 128
N_SEL = PEER_HEADS * PEER_TOPK
EPS = 1e-6
NEG = -1e30

P_Q = 0
P_K = 512
P_V = 768
P_B = 1024
P_C = 1536
P_U = 2048
P_W = 2560

CHUNKS = D_MODEL // LANES
WORD_ROWS = CHUNKS // 2

TOKEN_TILE = 512
ROUTER_UNROLL = 4
PEER_TILE = 256
PEER_TILES = 16

VMEM_LIMIT_SMALL = 48 * 1024 * 1024
VMEM_LIMIT_TABLE = 56 * 1024 * 1024


def _bf16(x):
    return x.astype(jnp.bfloat16)


def _dot(a, b):
    return jnp.dot(a, b, preferred_element_type=jnp.float32)


def _dot_nt(a, b):
    return lax.dot_general(a, b, (((1,), (1,)), ((), ())),
                           preferred_element_type=jnp.float32)


def _split2(x):
    hi = _bf16(x)
    lo = _bf16(x - hi.astype(jnp.float32))
    return hi, lo


def _split3(x):
    hi = _bf16(x)
    r = x - hi.astype(jnp.float32)
    mid = _bf16(r)
    lo = _bf16(r - mid.astype(jnp.float32))
    return hi, mid, lo


def _dot_sel(x, sel):
    hi, mid, lo = _split3(x)
    return _dot(hi, sel) + _dot(mid, sel) + _dot(lo, sel)


def _rms(x):
    return x * lax.rsqrt(jnp.mean(x * x, axis=-1, keepdims=True) + EPS)


def _ada_kernel(c_ref, w_ref, b_ref, o_ref):
    c = c_ref[...]
    a = c * (1.0 / (1.0 + jnp.exp(-c)))
    hi, mid, lo = _split3(a)
    w = w_ref[...]
    w_hi, w_mid, w_lo = _split3(w)
    acc = _dot(hi, w_hi)
    acc += _dot(hi, w_mid) + _dot(mid, w_hi)
    acc += _dot(hi, w_lo) + _dot(mid, w_mid) + _dot(lo, w_hi)
    o_ref[...] = acc + b_ref[...]


def _ada(c, w_ada, b_ada):
    bsz, d = c.shape
    n = w_ada.shape[1]
    blk = d
    return pl.pallas_call(
        _ada_kernel,
        grid=(n // blk,),
        in_specs=[
            pl.BlockSpec((bsz, d), lambda i: (0, 0)),
            pl.BlockSpec((d, blk), lambda i: (0, i)),
            pl.BlockSpec((1, blk), lambda i: (0, i)),
        ],
        out_specs=pl.BlockSpec((bsz, blk), lambda i: (0, i)),
        out_shape=jax.ShapeDtypeStruct((bsz, n), jnp.float32),
        compiler_params=pltpu.CompilerParams(
            dimension_semantics=("arbitrary",),
            vmem_limit_bytes=VMEM_LIMIT_SMALL),
        name="ada",
    )(c, w_ada, b_ada.reshape(1, n))


def _head_rms(o, gain, bd):
    hi, lo = _split2(o * o)
    ms = _dot(hi, bd) + _dot(lo, bd)
    return o * lax.rsqrt(ms + EPS) * gain


def _mixer_kernel(x_ref, sh_ref, sc_ref, gt_ref, g1_ref, win_ref, bin_ref,
                  sink_ref, cw_ref, ag_ref, cg_ref, wout_ref, bout_ref, bd_ref,
                  x1_ref, proj_ref, mix_ref, kc_ref, vc_ref, zc_ref):
    j = pl.program_id(1)
    tm = x_ref.shape[1]

    @pl.when(j == 0)
    def _():
        kc_ref[...] = jnp.zeros_like(kc_ref)
        vc_ref[...] = jnp.zeros_like(vc_ref)
        zc_ref[...] = jnp.zeros_like(zc_ref)

    x = x_ref[0]
    h = _rms(x) * g1_ref[...] * (1.0 + sc_ref[0]) + sh_ref[0]
    proj_ref[...] = _dot(_bf16(h), win_ref[...]) + bin_ref[...]

    qi = lax.broadcasted_iota(jnp.int32, (WINDOW, 2 * WINDOW), 0)
    kj = lax.broadcasted_iota(jnp.int32, (WINDOW, 2 * WINDOW), 1)
    valid = (kj > qi) & (kj <= qi + WINDOW)
    valid_first = valid & (kj >= jnp.where(j > 0, 0, WINDOW))
    lane_lo = lax.broadcasted_iota(jnp.int32, (1, LANES), 1) < HEAD_DIM
    scale = HEAD_DIM ** -0.5

    k_prev = _bf16(kc_ref[...])
    v_prev = _bf16(vc_ref[...])
    for n in range(tm // WINDOW):
        r0 = n * WINDOW
        k_cur = _bf16(proj_ref[r0:r0 + WINDOW, P_K:P_K + 2 * LANES])
        v_cur = _bf16(proj_ref[r0:r0 + WINDOW, P_V:P_V + 2 * LANES])
        kk = jnp.concatenate([k_prev, k_cur], axis=0)
        vv = jnp.concatenate([v_prev, v_cur], axis=0)
        ok = valid_first if n == 0 else valid
        for kvh in range(N_KV_HEADS):
            kt = kk[:, kvh * LANES:(kvh + 1) * LANES]
            vt = vv[:, kvh * LANES:(kvh + 1) * LANES]
            zero = jnp.zeros_like(kt)
            k_half = (jnp.where(lane_lo, kt, zero), jnp.where(lane_lo, zero, kt))
            v_half = (jnp.where(lane_lo, vt, zero), jnp.where(lane_lo, zero, vt))
            for a in (2 * kvh, 2 * kvh + 1):
                qt = _bf16(proj_ref[r0:r0 + WINDOW, P_Q + a * LANES:P_Q + (a + 1) * LANES])
                o_tile = None
                for par in range(2):
                    sink = sink_ref[2 * a + par]
                    s = _dot_nt(qt, k_half[par]) * scale
                    s = jnp.where(ok, s, NEG)
                    m = jnp.maximum(jnp.max(s, axis=-1, keepdims=True), sink)
                    e = jnp.exp(s - m)
                    den = jnp.sum(e, axis=-1, keepdims=True) + jnp.exp(sink - m)
                    p = _bf16(e * (1.0 / den))
                    o = _dot(p, v_half[par])
                    o_tile = o if o_tile is None else o_tile + o
                mix_ref[r0:r0 + WINDOW, a * LANES:(a + 1) * LANES] = o_tile
        k_prev, v_prev = k_cur, v_cur
    kc_ref[...] = k_prev.astype(jnp.float32)
    vc_ref[...] = v_prev.astype(jnp.float32)

    z = proj_ref[:, P_C:P_C + CONV_WIDTH] * proj_ref[:, P_U:P_U + CONV_WIDTH]
    zc = zc_ref[...]
    row8 = lax.broadcasted_iota(jnp.int32, (SUBLANES, CONV_WIDTH), 0)
    z1 = pltpu.roll(z, 1, axis=0)
    z2 = pltpu.roll(z, 2, axis=0)
    z1_top = jnp.where(row8 < 1, pltpu.roll(zc, 1, axis=0), z1[0:SUBLANES])
    z2_top = jnp.where(row8 < 2, pltpu.roll(zc, 2, axis=0), z2[0:SUBLANES])
    z1 = jnp.concatenate([z1_top, z1[SUBLANES:]], axis=0)
    z2 = jnp.concatenate([z2_top, z2[SUBLANES:]], axis=0)
    zc_ref[...] = z[tm - SUBLANES:tm]
    conv = z2 * cw_ref[0:1, :] + z1 * cw_ref[1:2, :] + z * cw_ref[2:3, :]
    conv = proj_ref[:, P_B:P_B + CONV_WIDTH] * conv

    bd = bd_ref[...]
    mix_ref[:, 0:ATTN_WIDTH] = _head_rms(mix_ref[:, 0:ATTN_WIDTH], ag_ref[...], bd)
    mix_ref[:, ATTN_WIDTH:] = _head_rms(conv, cg_ref[...], bd)
    mix = _dot(_bf16(mix_ref[...]), wout_ref[...]) + bout_ref[...]
    x1_ref[0] = x + gt_ref[0] * mix


def _mixer(x, shift1, scale1, gate1, norm1_g, w_in_r, b_in_r, sinks, conv_w,
           attn_out_g, conv_out_g, w_out_b, b_out):
    bsz, s_len, d = x.shape
    tm = min(TOKEN_TILE, s_len)
    bd = np.kron(np.eye(ATTN_WIDTH // HEAD_DIM, dtype=np.float32),
                 np.full((HEAD_DIM, HEAD_DIM), 1.0 / HEAD_DIM, np.float32))
    bd = jnp.asarray(bd, jnp.bfloat16)
    vec = lambda n: pl.BlockSpec((1, n), lambda b, j: (0, 0))
    per_b = pl.BlockSpec((1, 1, d), lambda b, j: (b, 0, 0))
    full = lambda shp: pl.BlockSpec(shp, lambda b, j: (0,) * len(shp))
    return pl.pallas_call(
        _mixer_kernel,
        grid=(bsz, s_len // tm),
        in_specs=[
            pl.BlockSpec((1, tm, d), lambda b, j: (b, j, 0)),
            per_b, per_b, per_b,
            vec(d),
            full((d, P_W)), vec(P_W),
            pl.BlockSpec(memory_space=pltpu.SMEM),
            full((3, CONV_WIDTH)),
            vec(ATTN_WIDTH), vec(CONV_WIDTH),
            full((d, d)), vec(d),
            full((ATTN_WIDTH, ATTN_WIDTH)),
        ],
        out_specs=pl.BlockSpec((1, tm, d), lambda b, j: (b, j, 0)),
        out_shape=jax.ShapeDtypeStruct((bsz, s_len, d), jnp.float32),
        scratch_shapes=[
            pltpu.VMEM((tm, P_W), jnp.float32),
            pltpu.VMEM((tm, d), jnp.float32),
            pltpu.VMEM((WINDOW, 2 * LANES), jnp.float32),
            pltpu.VMEM((WINDOW, 2 * LANES), jnp.float32),
            pltpu.VMEM((SUBLANES, CONV_WIDTH), jnp.float32),
        ],
        compiler_params=pltpu.CompilerParams(
            dimension_semantics=("arbitrary", "arbitrary"),
            vmem_limit_bytes=VMEM_LIMIT_SMALL),
        name="mixer",
    )(x, shift1.reshape(bsz, 1, d), scale1.reshape(bsz, 1, d), gate1.reshape(bsz, 1, d),
      norm1_g.reshape(1, d), w_in_r, b_in_r.reshape(1, P_W), sinks, conv_w,
      attn_out_g.reshape(1, ATTN_WIDTH), conv_out_g.reshape(1, CONV_WIDTH),
      w_out_b, b_out.reshape(1, d), bd)


def _staircase():
    flat = np.zeros((80,), np.int32)
    ok = np.zeros((80,), bool)
    for a in range(8):
        for b in range(8):
            flat[a * 8 + b] = a * PEER_TOPK + b
            ok[a * 8 + b] = (a + 1) * (b + 1) <= PEER_TOPK
    for r in range(8):
        flat[64 + r] = 8 + r
        ok[64 + r] = True
        flat[72 + r] = (8 + r) * PEER_TOPK
        ok[72 + r] = True
    return flat, ok


def _extract_topk(s, ids, big):
    vals, idxs = [], []
    for _ in range(PEER_TOPK):
        m = jnp.max(s, axis=0, keepdims=True)
        i = jnp.min(jnp.where(s == m, ids, big), axis=0, keepdims=True)
        vals.append(m)
        idxs.append(i)
        s = jnp.where(ids == i, -jnp.inf, s)
    return jnp.concatenate(vals, axis=0), jnp.concatenate(idxs, axis=0)


def _take_rows(table, sel):
    out = jnp.zeros_like(table)
    for a in range(PEER_TOPK):
        out = jnp.where(sel == a, table[a:a + 1, :], out)
    return out


def _router_kernel(x1_ref, sh_ref, sc_ref, g2_ref, wq_ref, k1_ref, k2_ref,
                   flat_ref, ok_ref, h2_ref, idx_ref, gate_ref,
                   st_ref, oi_ref, og_ref):
    tm = x1_ref.shape[1]
    nt = tm // LANES
    x1 = x1_ref[0]
    h2 = _rms(x1) * g2_ref[...] * (1.0 + sc_ref[0]) + sh_ref[0]
    h2_ref[0] = h2
    q = _dot(_bf16(h2), wq_ref[...])
    for hd in range(PEER_HEADS):
        for side, kref in enumerate((k1_ref, k2_ref)):
            c0 = hd * D_KEY + side * D_HALF
            st = _dot_nt(kref[hd], _bf16(q[:, c0:c0 + D_HALF]))
            for tt in range(nt):
                st_ref[2 * hd + side, tt] = st[:, tt * LANES:(tt + 1) * LANES]

    key_ids = lax.broadcasted_iota(jnp.int32, (N_KEYS, LANES), 0).astype(jnp.float32)
    flat = flat_ref[...]
    cand_ok = ok_ref[...] > 0

    def select(hd, tt):
        v1, i1 = _extract_topk(st_ref[2 * hd, tt], key_ids, float(N_KEYS))
        v2, i2 = _extract_topk(st_ref[2 * hd + 1, tt], key_ids, float(N_KEYS))
        parts = [v1[a:a + 1, :] + v2[0:SUBLANES, :] for a in range(SUBLANES)]
        parts.append(v1[0:1, :] + v2[SUBLANES:, :])
        parts.append(v1[SUBLANES:, :] + v2[0:1, :])
        cand = jnp.where(cand_ok, jnp.concatenate(parts, axis=0), -jnp.inf)
        sc, cf = _extract_topk(cand, flat, float(PEER_TOPK * PEER_TOPK))
        cf = cf.astype(jnp.int32)
        e1 = _take_rows(i1, cf >> 4).astype(jnp.int32)
        e2 = _take_rows(i2, cf & (PEER_TOPK - 1)).astype(jnp.int32)
        e = jnp.exp(sc - sc[0:1, :])
        g = e * (1.0 / jnp.sum(e, axis=0, keepdims=True))
        r0 = pl.multiple_of(hd * PEER_TOPK, PEER_TOPK)
        oi_ref[tt, pl.ds(r0, PEER_TOPK), :] = (e1 * N_KEYS + e2) * WORD_ROWS
        og_ref[tt, pl.ds(r0, PEER_TOPK), :] = g

    def body(n, carry):
        hd = lax.div(n, nt // ROUTER_UNROLL)
        t0 = lax.rem(n, nt // ROUTER_UNROLL) * ROUTER_UNROLL
        for k in range(ROUTER_UNROLL):
            select(hd, t0 + k)
        return carry

    lax.fori_loop(0, PEER_HEADS * nt // ROUTER_UNROLL, body, 0)
    for tt in range(nt):
        idx_ref[tt * LANES:(tt + 1) * LANES, :] = oi_ref[tt].T
        gate_ref[tt * LANES:(tt + 1) * LANES, :] = og_ref[tt].T


def _router(x1, shift2, scale2, norm2_g, w_query_b, keys1_b, keys2_b):
    bsz, s_len, d = x1.shape
    tm = min(TOKEN_TILE, s_len)
    nt = tm // LANES
    nj = s_len // tm
    flat, ok = _staircase()
    flat = jnp.asarray(np.broadcast_to(flat[:, None], (80, LANES)), jnp.float32)
    ok = jnp.asarray(np.broadcast_to(ok[:, None], (80, LANES)), jnp.int32)
    per_b = pl.BlockSpec((1, 1, d), lambda b, j: (b, 0, 0))
    full = lambda shp: pl.BlockSpec(shp, lambda b, j: (0,) * len(shp))
    tok = lambda b, j: (b * nj + j, 0)
    return pl.pallas_call(
        _router_kernel,
        grid=(bsz, nj),
        in_specs=[
            pl.BlockSpec((1, tm, d), lambda b, j: (b, j, 0)),
            per_b, per_b,
            full((1, d)),
            full((d, PEER_HEADS * D_KEY)),
            full((PEER_HEADS, N_KEYS, D_HALF)),
            full((PEER_HEADS, N_KEYS, D_HALF)),
            full((80, LANES)), full((80, LANES)),
        ],
        out_specs=[
            pl.BlockSpec((1, tm, d), lambda b, j: (b, j, 0)),
            pl.BlockSpec((tm, N_SEL), tok),
            pl.BlockSpec((tm, N_SEL), tok),
        ],
        out_shape=[
            jax.ShapeDtypeStruct((bsz, s_len, d), jnp.float32),
            jax.ShapeDtypeStruct((bsz * s_len, N_SEL), jnp.int32),
            jax.ShapeDtypeStruct((bsz * s_len, N_SEL), jnp.float32),
        ],
        scratch_shapes=[
            pltpu.VMEM((2 * PEER_HEADS, nt, N_KEYS, LANES), jnp.float32),
            pltpu.VMEM((nt, N_SEL, LANES), jnp.int32),
            pltpu.VMEM((nt, N_SEL, LANES), jnp.float32),
        ],
        compiler_params=pltpu.CompilerParams(
            dimension_semantics=("arbitrary", "arbitrary"),
            vmem_limit_bytes=VMEM_LIMIT_SMALL),
        name="router",
    )(x1, shift2.reshape(bsz, 1, d), scale2.reshape(bsz, 1, d), norm2_g.reshape(1, d),
      w_query_b, keys1_b, keys2_b, flat, ok)


def _pack_kernel(t_ref, o_ref):
    o_ref[...] = pltpu.bitcast(_bf16(t_ref[...]), jnp.int32)


def _pack_table(t):
    rows = t.shape[0] * CHUNKS
    blk = 4096
    return pl.pallas_call(
        _pack_kernel,
        grid=(rows // blk,),
        in_specs=[pl.BlockSpec((blk, LANES), lambda i: (i, 0))],
        out_specs=pl.BlockSpec((blk // 2, LANES), lambda i: (i, 0)),
        out_shape=jax.ShapeDtypeStruct((rows // 2, LANES), jnp.int32),
        compiler_params=pltpu.CompilerParams(
            dimension_semantics=("arbitrary",),
            vmem_limit_bytes=VMEM_LIMIT_SMALL),
        name="pack_table",
    )(t.reshape(rows, LANES))


def _gather_rows(idx_ref, tok, table_ref, tile_ref):
    for m0 in range(0, N_SEL, SUBLANES):
        ids = idx_ref.at[tok, pl.ds(m0, SUBLANES)]
        for k in range(SUBLANES):
            mi = m0 + k
            row = pl.multiple_of(ids[k], WORD_ROWS)
            tile_ref[mi * WORD_ROWS:(mi + 1) * WORD_ROWS, :] = table_ref[pl.ds(row, WORD_ROWS), :]


def _chunk_mask():
    r = lax.broadcasted_iota(jnp.int32, (CHUNKS, N_SEL * CHUNKS), 0)
    c = lax.broadcasted_iota(jnp.int32, (CHUNKS, N_SEL * CHUNKS), 1)
    return (c % CHUNKS) == r


def _token_pipeline(idx_ref, table_ref, tiles, n_tokens, consume):
    g = len(tiles) // 2
    group_a, group_b = tiles[:g], tiles[g:]

    def gather_group(t, group):
        for k in range(g):
            _gather_rows(idx_ref, t + k, table_ref, group[k])

    def consume_group(t, group):
        for k in range(g):
            consume(t + k, group[k])

    @pl.when(pl.program_id(0) == 0)
    def _():
        gather_group(0, group_a)

    always = idx_ref[0, 0] >= 0

    def body(p, carry):
        t0 = 2 * g * p

        @pl.when(always)
        def _():
            consume_group(t0, group_a)
            gather_group(t0 + g, group_b)

        @pl.when(always)
        def _():
            consume_group(t0 + g, group_b)
            gather_group(t0 + 2 * g, group_a)

        return carry

    lax.fori_loop(0, n_tokens // (2 * g), body, 0)


def _load_table(table_hbm, table_ref, sem):
    @pl.when(pl.program_id(0) == 0)
    def _():
        cp = pltpu.make_async_copy(table_hbm, table_ref, sem)
        cp.start()
        cp.wait()


def _peer_u_kernel(idx_ref, h_ref, gate_ref, table_hbm, gsum_ref, w_ref,
                   table_ref, *scratch):
    tiles, (d_ref, sem) = scratch[:PEER_TILES], scratch[PEER_TILES:]
    tb = h_ref.shape[0]
    _load_table(table_hbm, table_ref, sem)
    mask = _chunk_mask()

    def scores(t, tile_ref):
        hrow = h_ref[pl.ds(t, 1), :]
        h8 = jnp.concatenate([hrow[:, c * LANES:(c + 1) * LANES] for c in range(CHUNKS)], axis=0)
        hi, lo = _split2(h8)
        lhs = jnp.concatenate([hi, lo], axis=0)
        xt = pltpu.bitcast(tile_ref[...], jnp.bfloat16)
        y = _dot_nt(lhs, xt)
        yc = y[0:CHUNKS] + y[CHUNKS:]
        d_ref[pl.ds(t, 1), :] = jnp.sum(jnp.where(mask, yc, 0.0), axis=0, keepdims=True)

    _token_pipeline(idx_ref, table_ref, tiles, tb, scores)
    s = _dot_sel(d_ref[...], gsum_ref[...])
    act = 0.5 * s * (1.0 + lax.erf(s * (2.0 ** -0.5)))
    w_ref[...] = gate_ref[...] * act


def _peer_v_kernel(idx_ref, w_ref, table_hbm, expand_ref, x1_ref, gt_ref, fg_ref, o_ref,
                   table_ref, *scratch):
    tiles, (whi_ref, wlo_ref, sem) = scratch[:PEER_TILES], scratch[PEER_TILES:]
    tb = w_ref.shape[0]
    _load_table(table_hbm, table_ref, sem)

    w_hi, w_lo = _split2(w_ref[...])
    whi_ref[...] = _dot(w_hi, expand_ref[...])
    wlo_ref[...] = _dot(w_lo, expand_ref[...])
    mask = _chunk_mask()

    def combine(t, tile_ref):
        hi = jnp.broadcast_to(whi_ref[pl.ds(t, 1), :], mask.shape)
        lo = jnp.broadcast_to(wlo_ref[pl.ds(t, 1), :], mask.shape)
        lhs = jnp.concatenate([_bf16(jnp.where(mask, hi, 0.0)),
                               _bf16(jnp.where(mask, lo, 0.0))], axis=0)
        xt = pltpu.bitcast(tile_ref[...], jnp.bfloat16)
        o = _dot(lhs, xt)
        o8 = o[0:CHUNKS] + o[CHUNKS:]
        o_ref[pl.ds(t, 1), :] = jnp.concatenate([o8[c:c + 1, :] for c in range(CHUNKS)], axis=1)

    _token_pipeline(idx_ref, table_ref, tiles, tb, combine)
    x2 = x1_ref[...] + gt_ref[0] * o_ref[...]
    o_ref[...] = _rms(x2) * fg_ref[...]


def _extend_id_blocks(idx, tb):
    g = PEER_TILES // 2
    blocks = idx.reshape(-1, tb, N_SEL)
    nxt = jnp.concatenate([blocks[1:, :g], jnp.zeros((1, g, N_SEL), idx.dtype)], axis=0)
    return jnp.concatenate([blocks, nxt], axis=1).reshape(-1, N_SEL)


def _peer_scratch(tb):
    tile = pltpu.VMEM((N_SEL * WORD_ROWS, LANES), jnp.int32)
    return [pltpu.VMEM((N_EXPERTS * WORD_ROWS, LANES), jnp.int32)] + [tile] * PEER_TILES


def _peer_u(idx, h2, gates, table, tb):
    t = h2.shape[0]
    gsum = np.kron(np.eye(N_SEL, dtype=np.float32), np.ones((CHUNKS, 1), np.float32))
    gsum = jnp.asarray(gsum, jnp.bfloat16)
    return pl.pallas_call(
        _peer_u_kernel,
        grid=(t // tb,),
        in_specs=[
            pl.BlockSpec((tb + PEER_TILES // 2, N_SEL), lambda i: (i, 0), memory_space=pltpu.SMEM),
            pl.BlockSpec((tb, N_SEL * CHUNKS), lambda i: (i, 0)),
            pl.BlockSpec((tb, N_SEL), lambda i: (i, 0)),
            pl.BlockSpec(memory_space=pl.ANY),
            pl.BlockSpec((N_SEL * CHUNKS, N_SEL), lambda i: (0, 0)),
        ],
        out_specs=pl.BlockSpec((tb, N_SEL), lambda i: (i, 0)),
        out_shape=jax.ShapeDtypeStruct((t, N_SEL), jnp.float32),
        scratch_shapes=_peer_scratch(tb) + [
            pltpu.VMEM((tb, N_SEL * CHUNKS), jnp.float32),
            pltpu.SemaphoreType.DMA,
        ],
        compiler_params=pltpu.CompilerParams(
            dimension_semantics=("arbitrary",),
            vmem_limit_bytes=VMEM_LIMIT_TABLE),
        name="peer_u",
    )(idx, h2, gates, table, gsum)


def _peer_v(idx, w, table, x1, gate2, final_g, tb):
    bsz, s_len, d = x1.shape
    t = w.shape[0]
    expand = np.kron(np.eye(N_SEL, dtype=np.float32), np.ones((1, CHUNKS), np.float32))
    expand = jnp.asarray(expand, jnp.bfloat16)
    return pl.pallas_call(
        _peer_v_kernel,
        grid=(t // tb,),
        in_specs=[
            pl.BlockSpec((tb + PEER_TILES // 2, N_SEL), lambda i: (i, 0), memory_space=pltpu.SMEM),
            pl.BlockSpec((tb, N_SEL), lambda i: (i, 0)),
            pl.BlockSpec(memory_space=pl.ANY),
            pl.BlockSpec((N_SEL, N_SEL * CHUNKS), lambda i: (0, 0)),
            pl.BlockSpec((tb, d), lambda i: (i, 0)),
            pl.BlockSpec((1, 1, d), lambda i: (i * tb // s_len, 0, 0)),
            pl.BlockSpec((1, d), lambda i: (0, 0)),
        ],
        out_specs=pl.BlockSpec((tb, d), lambda i: (i, 0)),
        out_shape=jax.ShapeDtypeStruct((t, d), jnp.float32),
        scratch_shapes=_peer_scratch(tb) + [
            pltpu.VMEM((tb, N_SEL * CHUNKS), jnp.float32),
            pltpu.VMEM((tb, N_SEL * CHUNKS), jnp.float32),
            pltpu.SemaphoreType.DMA,
        ],
        compiler_params=pltpu.CompilerParams(
            dimension_semantics=("arbitrary",),
            vmem_limit_bytes=VMEM_LIMIT_TABLE),
        name="peer_v",
    )(idx, w, table, expand, x1.reshape(t, d), gate2.reshape(bsz, 1, d), final_g.reshape(1, d))


def _relayout_in_proj(w_in, b_in):
    def cols(a):
        q = a[..., 0:512]
        k = a[..., 512:640]
        v = a[..., 640:768]
        rest = a[..., 768:]
        dup = lambda t: jnp.concatenate(
            [t[..., 0:64], t[..., 0:64], t[..., 64:128], t[..., 64:128]], axis=-1)
        return jnp.concatenate([q, dup(k), dup(v), rest], axis=-1)
    return cols(w_in), cols(b_in)


def kernel(x, c, w_ada, b_ada, norm1_g, w_in, b_in, attn_sinks, conv_w, attn_out_g,
           conv_out_g, w_out, b_out, norm2_g, w_query, peer_keys1, peer_keys2,
           peer_u, peer_v, final_g):
    bsz, s_len, d = x.shape
    ada = _ada(c, w_ada, b_ada)
    shift1, scale1, gate1, shift2, scale2, gate2 = jnp.split(ada, 6, axis=-1)

    w_in_r, b_in_r = _relayout_in_proj(w_in, b_in)
    x1 = _mixer(x, shift1, scale1, gate1, norm1_g, _bf16(w_in_r), b_in_r, attn_sinks,
                conv_w, attn_out_g, conv_out_g, _bf16(w_out), b_out)

    h2, idx, gates = _router(x1, shift2, scale2, norm2_g, _bf16(w_query),
                             _bf16(peer_keys1), _bf16(peer_keys2))
    t = bsz * s_len
    tb = min(PEER_TILE, s_len)
    assert s_len % tb == 0 and tb % PEER_TILES == 0
    idx_ext = _extend_id_blocks(idx, tb)
    w = _peer_u(idx_ext, h2.reshape(t, d), gates, _pack_table(peer_u), tb)
    y = _peer_v(idx_ext, w, _pack_table(peer_v), x1, gate2, final_g, tb)
    return y.reshape(bsz, s_len, d)
```

```python
import jax
import jax.numpy as jnp
import numpy as np
from jax import lax
from jax.experimental import pallas as pl
from jax.experimental.pallas import tpu as pltpu

LANES = 128
SUBLANES = 8

D_MODEL = 1024
HEAD_DIM = 64
ATTN_WIDTH = 512
CONV_WIDTH = 512
N_HEADS = 8
N_KV_HEADS = 2
WINDOW = 128
N_KEYS = 128
N_EXPERTS = N_KEYS * N_KEYS
PEER_HEADS = 8
PEER_TOPK = 16
D_KEY = 256
D_HALF = 128
N_SEL = PEER_HEADS * PEER_TOPK
EPS = 1e-6
NEG = -1e30

P_Q = 0
P_K = 512
P_V = 768
P_B = 1024
P_C = 1536
P_U = 2048
P_W = 2560

CHUNKS = D_MODEL // LANES
WORD_ROWS = CHUNKS // 2

TOKEN_TILE = 512
N_CAND_ROWS = 10 * SUBLANES
TOPK_SHIFT = 4
ID_GROUP = 16
PACK_ROWS = 4096
ROUTER_UNROLL = 4
PEER_TILE = 256
PEER_TILES = 16

VMEM_LIMIT_SMALL = 48 * 1024 * 1024
VMEM_LIMIT_TABLE = 56 * 1024 * 1024


def _bf16(x):
    return x.astype(jnp.bfloat16)


def _dot(a, b):
    return jnp.dot(a, b, preferred_element_type=jnp.float32)


def _dot_nt(a, b):
    return lax.dot_general(a, b, (((1,), (1,)), ((), ())),
                           preferred_element_type=jnp.float32)


def _split2(x):
    hi = _bf16(x)
    lo = _bf16(x - hi.astype(jnp.float32))
    return hi, lo


def _split3(x):
    hi = _bf16(x)
    r = x - hi.astype(jnp.float32)
    mid = _bf16(r)
    lo = _bf16(r - mid.astype(jnp.float32))
    return hi, mid, lo


def _dot_sel(x, sel):
    hi, mid, lo = _split3(x)
    return _dot(hi, sel) + _dot(mid, sel) + _dot(lo, sel)


def _rms(x):
    return x * lax.rsqrt(jnp.mean(x * x, axis=-1, keepdims=True) + EPS)


def _ada_kernel(c_ref, w_ref, b_ref, o_ref):
    c = c_ref[...]
    a = c * (1.0 / (1.0 + jnp.exp(-c)))
    hi, mid, lo = _split3(a)
    w = w_ref[...]
    w_hi, w_mid, w_lo = _split3(w)
    acc = _dot(hi, w_hi)
    acc += _dot(hi, w_mid) + _dot(mid, w_hi)
    acc += _dot(hi, w_lo) + _dot(mid, w_mid) + _dot(lo, w_hi)
    o_ref[...] = acc + b_ref[...]


def _ada(c, w_ada, b_ada):
    bsz, d = c.shape
    n = w_ada.shape[1]
    blk = d
    return pl.pallas_call(
        _ada_kernel,
        grid=(n // blk,),
        in_specs=[
            pl.BlockSpec((bsz, d), lambda i: (0, 0)),
            pl.BlockSpec((d, blk), lambda i: (0, i)),
            pl.BlockSpec((1, blk), lambda i: (0, i)),
        ],
        out_specs=pl.BlockSpec((bsz, blk), lambda i: (0, i)),
        out_shape=jax.ShapeDtypeStruct((bsz, n), jnp.float32),
        compiler_params=pltpu.CompilerParams(
            dimension_semantics=("arbitrary",),
            vmem_limit_bytes=VMEM_LIMIT_SMALL),
        name="ada",
    )(c, w_ada, b_ada.reshape(1, n))


def _head_rms(o, gain, bd):
    hi, lo = _split2(o * o)
    ms = _dot(hi, bd) + _dot(lo, bd)
    return o * lax.rsqrt(ms + EPS) * gain


def _mixer_kernel(x_ref, sh_ref, sc_ref, gt_ref, g1_ref, win_ref, bin_ref,
                  sink_ref, cw_ref, ag_ref, cg_ref, wout_ref, bout_ref, bd_ref,
                  x1_ref, proj_ref, mix_ref, kc_ref, vc_ref, zc_ref):
    j = pl.program_id(1)
    tm = x_ref.shape[1]

    @pl.when(j == 0)
    def _():
        kc_ref[...] = jnp.zeros_like(kc_ref)
        vc_ref[...] = jnp.zeros_like(vc_ref)
        zc_ref[...] = jnp.zeros_like(zc_ref)

    x = x_ref[0]
    h = _rms(x) * g1_ref[...] * (1.0 + sc_ref[0]) + sh_ref[0]
    proj_ref[...] = _dot(_bf16(h), win_ref[...]) + bin_ref[...]

    qi = lax.broadcasted_iota(jnp.int32, (WINDOW, 2 * WINDOW), 0)
    kj = lax.broadcasted_iota(jnp.int32, (WINDOW, 2 * WINDOW), 1)
    valid = (kj > qi) & (kj <= qi + WINDOW)
    valid_first = valid & (kj >= jnp.where(j > 0, 0, WINDOW))
    lane_lo = lax.broadcasted_iota(jnp.int32, (1, LANES), 1) < HEAD_DIM
    scale = HEAD_DIM ** -0.5

    k_prev = _bf16(kc_ref[...])
    v_prev = _bf16(vc_ref[...])
    for n in range(tm // WINDOW):
        r0 = n * WINDOW
        k_cur = _bf16(proj_ref[r0:r0 + WINDOW, P_K:P_K + 2 * LANES])
        v_cur = _bf16(proj_ref[r0:r0 + WINDOW, P_V:P_V + 2 * LANES])
        kk = jnp.concatenate([k_prev, k_cur], axis=0)
        vv = jnp.concatenate([v_prev, v_cur], axis=0)
        ok = valid_first if n == 0 else valid
        for kvh in range(N_KV_HEADS):
            kt = kk[:, kvh * LANES:(kvh + 1) * LANES]
            vt = vv[:, kvh * LANES:(kvh + 1) * LANES]
            zero = jnp.zeros_like(kt)
            k_half = (jnp.where(lane_lo, kt, zero), jnp.where(lane_lo, zero, kt))
            v_half = (jnp.where(lane_lo, vt, zero), jnp.where(lane_lo, zero, vt))
            for a in (2 * kvh, 2 * kvh + 1):
                qt = _bf16(proj_ref[r0:r0 + WINDOW, P_Q + a * LANES:P_Q + (a + 1) * LANES])
                o_tile = None
                for par in range(2):
                    sink = sink_ref[2 * a + par]
                    s = _dot_nt(qt, k_half[par]) * scale
                    s = jnp.where(ok, s, NEG)
                    m = jnp.maximum(jnp.max(s, axis=-1, keepdims=True), sink)
                    e = jnp.exp(s - m)
                    den = jnp.sum(e, axis=-1, keepdims=True) + jnp.exp(sink - m)
                    p = _bf16(e * (1.0 / den))
                    o = _dot(p, v_half[par])
                    o_tile = o if o_tile is None else o_tile + o
                mix_ref[r0:r0 + WINDOW, a * LANES:(a + 1) * LANES] = o_tile
        k_prev, v_prev = k_cur, v_cur
    kc_ref[...] = k_prev.astype(jnp.float32)
    vc_ref[...] = v_prev.astype(jnp.float32)

    z = proj_ref[:, P_C:P_C + CONV_WIDTH] * proj_ref[:, P_U:P_U + CONV_WIDTH]
    zc = zc_ref[...]
    row8 = lax.broadcasted_iota(jnp.int32, (SUBLANES, CONV_WIDTH), 0)
    z1 = pltpu.roll(z, 1, axis=0)
    z2 = pltpu.roll(z, 2, axis=0)
    z1_top = jnp.where(row8 < 1, pltpu.roll(zc, 1, axis=0), z1[0:SUBLANES])
    z2_top = jnp.where(row8 < 2, pltpu.roll(zc, 2, axis=0), z2[0:SUBLANES])
    z1 = jnp.concatenate([z1_top, z1[SUBLANES:]], axis=0)
    z2 = jnp.concatenate([z2_top, z2[SUBLANES:]], axis=0)
    zc_ref[...] = z[tm - SUBLANES:tm]
    conv = z2 * cw_ref[0:1, :] + z1 * cw_ref[1:2, :] + z * cw_ref[2:3, :]
    conv = proj_ref[:, P_B:P_B + CONV_WIDTH] * conv

    bd = bd_ref[...]
    mix_ref[:, 0:ATTN_WIDTH] = _head_rms(mix_ref[:, 0:ATTN_WIDTH], ag_ref[...], bd)
    mix_ref[:, ATTN_WIDTH:] = _head_rms(conv, cg_ref[...], bd)
    mix = _dot(_bf16(mix_ref[...]), wout_ref[...]) + bout_ref[...]
    x1_ref[0] = x + gt_ref[0] * mix


def _mixer(x, shift1, scale1, gate1, norm1_g, w_in_r, b_in_r, sinks, conv_w,
           attn_out_g, conv_out_g, w_out_b, b_out):
    bsz, s_len, d = x.shape
    tm = min(TOKEN_TILE, s_len)
    bd = np.kron(np.eye(ATTN_WIDTH // HEAD_DIM, dtype=np.float32),
                 np.full((HEAD_DIM, HEAD_DIM), 1.0 / HEAD_DIM, np.float32))
    bd = jnp.asarray(bd, jnp.bfloat16)
    vec = lambda n: pl.BlockSpec((1, n), lambda b, j: (0, 0))
    per_b = pl.BlockSpec((1, 1, d), lambda b, j: (b, 0, 0))
    full = lambda shp: pl.BlockSpec(shp, lambda b, j: (0,) * len(shp))
    return pl.pallas_call(
        _mixer_kernel,
        grid=(bsz, s_len // tm),
        in_specs=[
            pl.BlockSpec((1, tm, d), lambda b, j: (b, j, 0)),
            per_b, per_b, per_b,
            vec(d),
            full((d, P_W)), vec(P_W),
            pl.BlockSpec(memory_space=pltpu.SMEM),
            full((3, CONV_WIDTH)),
            vec(ATTN_WIDTH), vec(CONV_WIDTH),
            full((d, d)), vec(d),
            full((ATTN_WIDTH, ATTN_WIDTH)),
        ],
        out_specs=pl.BlockSpec((1, tm, d), lambda b, j: (b, j, 0)),
        out_shape=jax.ShapeDtypeStruct((bsz, s_len, d), jnp.float32),
        scratch_shapes=[
            pltpu.VMEM((tm, P_W), jnp.float32),
            pltpu.VMEM((tm, d), jnp.float32),
            pltpu.VMEM((WINDOW, 2 * LANES), jnp.float32),
            pltpu.VMEM((WINDOW, 2 * LANES), jnp.float32),
            pltpu.VMEM((SUBLANES, CONV_WIDTH), jnp.float32),
        ],
        compiler_params=pltpu.CompilerParams(
            dimension_semantics=("arbitrary", "arbitrary"),
            vmem_limit_bytes=VMEM_LIMIT_SMALL),
        name="mixer",
    )(x, shift1.reshape(bsz, 1, d), scale1.reshape(bsz, 1, d), gate1.reshape(bsz, 1, d),
      norm1_g.reshape(1, d), w_in_r, b_in_r.reshape(1, P_W), sinks, conv_w,
      attn_out_g.reshape(1, ATTN_WIDTH), conv_out_g.reshape(1, CONV_WIDTH),
      w_out_b, b_out.reshape(1, d), bd)


def _staircase():
    flat = np.zeros((N_CAND_ROWS,), np.int32)
    ok = np.zeros((N_CAND_ROWS,), bool)
    for a in range(8):
        for b in range(8):
            flat[a * 8 + b] = a * PEER_TOPK + b
            ok[a * 8 + b] = (a + 1) * (b + 1) <= PEER_TOPK
    for r in range(8):
        flat[64 + r] = 8 + r
        ok[64 + r] = True
        flat[72 + r] = (8 + r) * PEER_TOPK
        ok[72 + r] = True
    return flat, ok


def _extract_topk(s, ids, big):
    vals, idxs = [], []
    for _ in range(PEER_TOPK):
        m = jnp.max(s, axis=0, keepdims=True)
        i = jnp.min(jnp.where(s == m, ids, big), axis=0, keepdims=True)
        vals.append(m)
        idxs.append(i)
        s = jnp.where(ids == i, -jnp.inf, s)
    return jnp.concatenate(vals, axis=0), jnp.concatenate(idxs, axis=0)


def _take_rows(table, sel):
    out = jnp.zeros_like(table)
    for a in range(PEER_TOPK):
        out = jnp.where(sel == a, table[a:a + 1, :], out)
    return out


def _router_kernel(x1_ref, sh_ref, sc_ref, g2_ref, wq_ref, k1_ref, k2_ref,
                   flat_ref, ok_ref, h2_ref, idx_ref, gate_ref,
                   st_ref, oi_ref, og_ref):
    tm = x1_ref.shape[1]
    nt = tm // LANES
    x1 = x1_ref[0]
    h2 = _rms(x1) * g2_ref[...] * (1.0 + sc_ref[0]) + sh_ref[0]
    h2_ref[0] = h2
    q = _dot(_bf16(h2), wq_ref[...])
    for hd in range(PEER_HEADS):
        for side, kref in enumerate((k1_ref, k2_ref)):
            c0 = hd * D_KEY + side * D_HALF
            st = _dot_nt(kref[hd], _bf16(q[:, c0:c0 + D_HALF]))
            for tt in range(nt):
                st_ref[2 * hd + side, tt] = st[:, tt * LANES:(tt + 1) * LANES]

    key_ids = lax.broadcasted_iota(jnp.int32, (N_KEYS, LANES), 0).astype(jnp.float32)
    flat = flat_ref[...]
    cand_ok = ok_ref[...] > 0

    def select(hd, tt):
        v1, i1 = _extract_topk(st_ref[2 * hd, tt], key_ids, float(N_KEYS))
        v2, i2 = _extract_topk(st_ref[2 * hd + 1, tt], key_ids, float(N_KEYS))
        parts = [v1[a:a + 1, :] + v2[0:SUBLANES, :] for a in range(SUBLANES)]
        parts.append(v1[0:1, :] + v2[SUBLANES:, :])
        parts.append(v1[SUBLANES:, :] + v2[0:1, :])
        cand = jnp.where(cand_ok, jnp.concatenate(parts, axis=0), -jnp.inf)
        sc, cf = _extract_topk(cand, flat, float(PEER_TOPK * PEER_TOPK))
        cf = cf.astype(jnp.int32)
        e1 = _take_rows(i1, cf >> TOPK_SHIFT).astype(jnp.int32)
        e2 = _take_rows(i2, cf & (PEER_TOPK - 1)).astype(jnp.int32)
        e = jnp.exp(sc - sc[0:1, :])
        g = e * (1.0 / jnp.sum(e, axis=0, keepdims=True))
        r0 = pl.multiple_of(hd * PEER_TOPK, PEER_TOPK)
        oi_ref[tt, pl.ds(r0, PEER_TOPK), :] = (e1 * N_KEYS + e2) * WORD_ROWS
        og_ref[tt, pl.ds(r0, PEER_TOPK), :] = g

    def body(n, carry):
        hd = lax.div(n, nt // ROUTER_UNROLL)
        t0 = lax.rem(n, nt // ROUTER_UNROLL) * ROUTER_UNROLL
        for k in range(ROUTER_UNROLL):
            select(hd, t0 + k)
        return carry

    lax.fori_loop(0, PEER_HEADS * nt // ROUTER_UNROLL, body, 0)
    for tt in range(nt):
        idx_ref[tt * LANES:(tt + 1) * LANES, :] = oi_ref[tt].T
        gate_ref[tt * LANES:(tt + 1) * LANES, :] = og_ref[tt].T


def _router(x1, shift2, scale2, norm2_g, w_query_b, keys1_b, keys2_b):
    bsz, s_len, d = x1.shape
    tm = min(TOKEN_TILE, s_len)
    nt = tm // LANES
    nj = s_len // tm
    flat, ok = _staircase()
    flat = jnp.asarray(np.broadcast_to(flat[:, None], (N_CAND_ROWS, LANES)), jnp.float32)
    ok = jnp.asarray(np.broadcast_to(ok[:, None], (N_CAND_ROWS, LANES)), jnp.int32)
    per_b = pl.BlockSpec((1, 1, d), lambda b, j: (b, 0, 0))
    full = lambda shp: pl.BlockSpec(shp, lambda b, j: (0,) * len(shp))
    tok = lambda b, j: (b * nj + j, 0)
    return pl.pallas_call(
        _router_kernel,
        grid=(bsz, nj),
        in_specs=[
            pl.BlockSpec((1, tm, d), lambda b, j: (b, j, 0)),
            per_b, per_b,
            full((1, d)),
            full((d, PEER_HEADS * D_KEY)),
            full((PEER_HEADS, N_KEYS, D_HALF)),
            full((PEER_HEADS, N_KEYS, D_HALF)),
            full((N_CAND_ROWS, LANES)), full((N_CAND_ROWS, LANES)),
        ],
        out_specs=[
            pl.BlockSpec((1, tm, d), lambda b, j: (b, j, 0)),
            pl.BlockSpec((tm, N_SEL), tok),
            pl.BlockSpec((tm, N_SEL), tok),
        ],
        out_shape=[
            jax.ShapeDtypeStruct((bsz, s_len, d), jnp.float32),
            jax.ShapeDtypeStruct((bsz * s_len, N_SEL), jnp.int32),
            jax.ShapeDtypeStruct((bsz * s_len, N_SEL), jnp.float32),
        ],
        scratch_shapes=[
            pltpu.VMEM((2 * PEER_HEADS, nt, N_KEYS, LANES), jnp.float32),
            pltpu.VMEM((nt, N_SEL, LANES), jnp.int32),
            pltpu.VMEM((nt, N_SEL, LANES), jnp.float32),
        ],
        compiler_params=pltpu.CompilerParams(
            dimension_semantics=("arbitrary", "arbitrary"),
            vmem_limit_bytes=VMEM_LIMIT_SMALL),
        name="router",
    )(x1, shift2.reshape(bsz, 1, d), scale2.reshape(bsz, 1, d), norm2_g.reshape(1, d),
      w_query_b, keys1_b, keys2_b, flat, ok)


def _pack_kernel(t_ref, o_ref):
    o_ref[...] = pltpu.bitcast(_bf16(t_ref[...]), jnp.int32)


def _pack_table(t):
    rows = t.shape[0] * CHUNKS
    blk = PACK_ROWS
    return pl.pallas_call(
        _pack_kernel,
        grid=(rows // blk,),
        in_specs=[pl.BlockSpec((blk, LANES), lambda i: (i, 0))],
        out_specs=pl.BlockSpec((blk // 2, LANES), lambda i: (i, 0)),
        out_shape=jax.ShapeDtypeStruct((rows // 2, LANES), jnp.int32),
        compiler_params=pltpu.CompilerParams(
            dimension_semantics=("arbitrary",),
            vmem_limit_bytes=VMEM_LIMIT_SMALL),
        name="pack_table",
    )(t.reshape(rows, LANES))


def _gather_rows(idx_ref, tok, table_ref, tile_ref):
    for m0 in range(0, N_SEL, ID_GROUP):
        ids = idx_ref.at[tok, pl.ds(m0, ID_GROUP)]
        for k in range(ID_GROUP):
            mi = m0 + k
            row = pl.multiple_of(ids[k], WORD_ROWS)
            tile_ref[mi * WORD_ROWS:(mi + 1) * WORD_ROWS, :] = table_ref[pl.ds(row, WORD_ROWS), :]


def _chunk_mask():
    r = lax.broadcasted_iota(jnp.int32, (CHUNKS, N_SEL * CHUNKS), 0)
    c = lax.broadcasted_iota(jnp.int32, (CHUNKS, N_SEL * CHUNKS), 1)
    return (c % CHUNKS) == r


def _token_pipeline(idx_ref, table_ref, tiles, n_tokens, consume):
    g = len(tiles) // 2
    group_a, group_b = tiles[:g], tiles[g:]

    def gather_group(t, group):
        for k in range(g):
            _gather_rows(idx_ref, t + k, table_ref, group[k])

    def consume_group(t, group):
        for k in range(g):
            consume(t + k, group[k])

    @pl.when(pl.program_id(0) == 0)
    def _():
        gather_group(0, group_a)

    always = idx_ref[0, 0] >= 0

    def body(p, carry):
        t0 = 2 * g * p

        @pl.when(always)
        def _():
            consume_group(t0, group_a)
            gather_group(t0 + g, group_b)

        @pl.when(always)
        def _():
            consume_group(t0 + g, group_b)
            gather_group(t0 + 2 * g, group_a)

        return carry

    lax.fori_loop(0, n_tokens // (2 * g), body, 0)


def _load_table(table_hbm, table_ref, sem):
    @pl.when(pl.program_id(0) == 0)
    def _():
        cp = pltpu.make_async_copy(table_hbm, table_ref, sem)
        cp.start()
        cp.wait()


def _peer_u_kernel(idx_ref, h_ref, gate_ref, table_hbm, gsum_ref, w_ref,
                   table_ref, *scratch):
    tiles, (d_ref, sem) = scratch[:PEER_TILES], scratch[PEER_TILES:]
    tb = h_ref.shape[0]
    _load_table(table_hbm, table_ref, sem)
    mask = _chunk_mask()

    def scores(t, tile_ref):
        hrow = h_ref[pl.ds(t, 1), :]
        h8 = jnp.concatenate([hrow[:, c * LANES:(c + 1) * LANES] for c in range(CHUNKS)], axis=0)
        hi, lo = _split2(h8)
        lhs = jnp.concatenate([hi, lo], axis=0)
        xt = pltpu.bitcast(tile_ref[...], jnp.bfloat16)
        y = _dot_nt(lhs, xt)
        yc = y[0:CHUNKS] + y[CHUNKS:]
        d_ref[pl.ds(t, 1), :] = jnp.sum(jnp.where(mask, yc, 0.0), axis=0, keepdims=True)

    _token_pipeline(idx_ref, table_ref, tiles, tb, scores)
    s = _dot_sel(d_ref[...], gsum_ref[...])
    act = 0.5 * s * (1.0 + lax.erf(s * (2.0 ** -0.5)))
    w_ref[...] = gate_ref[...] * act


def _peer_v_kernel(idx_ref, w_ref, table_hbm, expand_ref, x1_ref, gt_ref, fg_ref, o_ref,
                   table_ref, *scratch):
    tiles, (whi_ref, wlo_ref, sem) = scratch[:PEER_TILES], scratch[PEER_TILES:]
    tb = w_ref.shape[0]
    _load_table(table_hbm, table_ref, sem)

    w_hi, w_lo = _split2(w_ref[...])
    whi_ref[...] = _dot(w_hi, expand_ref[...])
    wlo_ref[...] = _dot(w_lo, expand_ref[...])
    mask = _chunk_mask()

    def combine(t, tile_ref):
        hi = jnp.broadcast_to(whi_ref[pl.ds(t, 1), :], mask.shape)
        lo = jnp.broadcast_to(wlo_ref[pl.ds(t, 1), :], mask.shape)
        lhs = jnp.concatenate([_bf16(jnp.where(mask, hi, 0.0)),
                               _bf16(jnp.where(mask, lo, 0.0))], axis=0)
        xt = pltpu.bitcast(tile_ref[...], jnp.bfloat16)
        o = _dot(lhs, xt)
        o8 = o[0:CHUNKS] + o[CHUNKS:]
        o_ref[pl.ds(t, 1), :] = jnp.concatenate([o8[c:c + 1, :] for c in range(CHUNKS)], axis=1)

    _token_pipeline(idx_ref, table_ref, tiles, tb, combine)
    x2 = x1_ref[...] + gt_ref[0] * o_ref[...]
    o_ref[...] = _rms(x2) * fg_ref[...]


def _extend_id_blocks(idx, tb):
    g = PEER_TILES // 2
    blocks = idx.reshape(-1, tb, N_SEL)
    nxt = jnp.concatenate([blocks[1:, :g], jnp.zeros((1, g, N_SEL), idx.dtype)], axis=0)
    return jnp.concatenate([blocks, nxt], axis=1).reshape(-1, N_SEL)


def _peer_scratch(tb):
    tile = pltpu.VMEM((N_SEL * WORD_ROWS, LANES), jnp.int32)
    return [pltpu.VMEM((N_EXPERTS * WORD_ROWS, LANES), jnp.int32)] + [tile] * PEER_TILES


def _peer_u(idx, h2, gates, table, tb):
    t = h2.shape[0]
    gsum = np.kron(np.eye(N_SEL, dtype=np.float32), np.ones((CHUNKS, 1), np.float32))
    gsum = jnp.asarray(gsum, jnp.bfloat16)
    return pl.pallas_call(
        _peer_u_kernel,
        grid=(t // tb,),
        in_specs=[
            pl.BlockSpec((tb + PEER_TILES // 2, N_SEL), lambda i: (i, 0), memory_space=pltpu.SMEM),
            pl.BlockSpec((tb, N_SEL * CHUNKS), lambda i: (i, 0)),
            pl.BlockSpec((tb, N_SEL), lambda i: (i, 0)),
            pl.BlockSpec(memory_space=pl.ANY),
            pl.BlockSpec((N_SEL * CHUNKS, N_SEL), lambda i: (0, 0)),
        ],
        out_specs=pl.BlockSpec((tb, N_SEL), lambda i: (i, 0)),
        out_shape=jax.ShapeDtypeStruct((t, N_SEL), jnp.float32),
        scratch_shapes=_peer_scratch(tb) + [
            pltpu.VMEM((tb, N_SEL * CHUNKS), jnp.float32),
            pltpu.SemaphoreType.DMA,
        ],
        compiler_params=pltpu.CompilerParams(
            dimension_semantics=("arbitrary",),
            vmem_limit_bytes=VMEM_LIMIT_TABLE),
        name="peer_u",
    )(idx, h2, gates, table, gsum)


def _peer_v(idx, w, table, x1, gate2, final_g, tb):
    bsz, s_len, d = x1.shape
    t = w.shape[0]
    expand = np.kron(np.eye(N_SEL, dtype=np.float32), np.ones((1, CHUNKS), np.float32))
    expand = jnp.asarray(expand, jnp.bfloat16)
    return pl.pallas_call(
        _peer_v_kernel,
        grid=(t // tb,),
        in_specs=[
            pl.BlockSpec((tb + PEER_TILES // 2, N_SEL), lambda i: (i, 0), memory_space=pltpu.SMEM),
            pl.BlockSpec((tb, N_SEL), lambda i: (i, 0)),
            pl.BlockSpec(memory_space=pl.ANY),
            pl.BlockSpec((N_SEL, N_SEL * CHUNKS), lambda i: (0, 0)),
            pl.BlockSpec((tb, d), lambda i: (i, 0)),
            pl.BlockSpec((1, 1, d), lambda i: (i * tb // s_len, 0, 0)),
            pl.BlockSpec((1, d), lambda i: (0, 0)),
        ],
        out_specs=pl.BlockSpec((tb, d), lambda i: (i, 0)),
        out_shape=jax.ShapeDtypeStruct((t, d), jnp.float32),
        scratch_shapes=_peer_scratch(tb) + [
            pltpu.VMEM((tb, N_SEL * CHUNKS), jnp.float32),
            pltpu.VMEM((tb, N_SEL * CHUNKS), jnp.float32),
            pltpu.SemaphoreType.DMA,
        ],
        compiler_params=pltpu.CompilerParams(
            dimension_semantics=("arbitrary",),
            vmem_limit_bytes=VMEM_LIMIT_TABLE),
        name="peer_v",
    )(idx, w, table, expand, x1.reshape(t, d), gate2.reshape(bsz, 1, d), final_g.reshape(1, d))


def _relayout_in_proj(w_in, b_in):
    def cols(a):
        q = a[..., 0:512]
        k = a[..., 512:640]
        v = a[..., 640:768]
        rest = a[..., 768:]
        dup = lambda t: jnp.concatenate(
            [t[..., 0:64], t[..., 0:64], t[..., 64:128], t[..., 64:128]], axis=-1)
        return jnp.concatenate([q, dup(k), dup(v), rest], axis=-1)
    return cols(w_in), cols(b_in)


def kernel(x, c, w_ada, b_ada, norm1_g, w_in, b_in, attn_sinks, conv_w, attn_out_g,
           conv_out_g, w_out, b_out, norm2_g, w_query, peer_keys1, peer_keys2,
           peer_u, peer_v, final_g):
    bsz, s_len, d = x.shape
    ada = _ada(c, w_ada, b_ada)
    shift1, scale1, gate1, shift2, scale2, gate2 = jnp.split(ada, 6, axis=-1)

    w_in_r, b_in_r = _relayout_in_proj(w_in, b_in)
    x1 = _mixer(x, shift1, scale1, gate1, norm1_g, _bf16(w_in_r), b_in_r, attn_sinks,
                conv_w, attn_out_g, conv_out_g, _bf16(w_out), b_out)

    h2, idx, gates = _router(x1, shift2, scale2, norm2_g, _bf16(w_query),
                             _bf16(peer_keys1), _bf16(peer_keys2))
    t = bsz * s_len
    tb = min(PEER_TILE, s_len)
    assert s_len % tb == 0 and tb % PEER_TILES == 0
    idx_ext = _extend_id_blocks(idx, tb)
    w = _peer_u(idx_ext, h2.reshape(t, d), gates, _pack_table(peer_u), tb)
    y = _peer_v(idx_ext, w, _pack_table(peer_v), x1, gate2, final_g, tb)
    return y.reshape(bsz, s_len, d)
```
